```python
import math
import jax, jax.numpy as jnp
from jax import lax
import numpy as np

D_MODEL = 4096
BATCH = 4
SEQ = 4096
DEPTH = 1

HEAD_DIM = 128
MIX_W = D_MODEL
ATT_W = MIX_W // 2
CONV_W = MIX_W - ATT_W
N_ATT_HEADS = ATT_W // HEAD_DIM
N_CONV_GROUPS = CONV_W // HEAD_DIM
IN_COLS = 3 * ATT_W + 3 * CONV_W
D_FF = 11008
BLOCK = 256
TOPK = 3
QSUB = 16
CONV_WIDTH = 3
N_BUCKETS = 32
MAX_DISTANCE = 2048
N_ADA = 9
EPS = 1e-6
NEG = -1e30

kernel_name = "hybrid_moba_shortconv_macaron_adaln"


def _rmsnorm(x, w):
    x32 = x.astype(jnp.float32)
    y = x32 * lax.rsqrt(jnp.mean(x32 * x32, axis=-1, keepdims=True) + EPS)
    return (y * w.astype(jnp.float32)).astype(x.dtype)


def _modulate(h, shift, scale):
    return h * (1.0 + scale) + shift


def _swiglu(h, w_gate, w_up, w_down):
    return (jax.nn.silu(h @ w_gate) * (h @ w_up)) @ w_down


def _t5_bucket(dist):
    max_exact = N_BUCKETS // 2
    d = jnp.maximum(dist, 1).astype(jnp.float32)
    large = max_exact + (jnp.log(d / max_exact) / math.log(MAX_DISTANCE / max_exact)
                         * (N_BUCKETS - max_exact)).astype(jnp.int32)
    large = jnp.minimum(large, N_BUCKETS - 1)
    return jnp.where(dist < max_exact, dist, large)


_gather_blocks = jax.vmap(jax.vmap(lambda blk, ix: blk[ix]))


def _moba_attention(q, k, v, rel_bias):
    B, H, Sp, hd = q.shape
    nb = Sp // BLOCK
    nsub = BLOCK // QSUB
    scale = hd ** -0.5
    kb = k.reshape(B, H, nb, BLOCK, hd)
    vb = v.reshape(B, H, nb, BLOCK, hd)
    kmean = jnp.mean(kb.astype(jnp.float32), axis=3).astype(k.dtype)
    table = rel_bias.astype(jnp.float32)
    table_t = table.T
    loc = jnp.arange(BLOCK, dtype=jnp.int32)
    own_dist = loc[:, None] - loc[None, :]
    own_bias = table[_t5_bucket(jnp.maximum(own_dist, 0))].transpose(2, 0, 1)
    own_bias = jnp.where(own_dist[None] >= 0, own_bias, NEG)
    own_bias_sub = own_bias.reshape(H, nsub, QSUB, BLOCK).transpose(1, 0, 2, 3)
    head_ix = jnp.arange(H)[None, :, None, None, None]

    outs = []
    for i in range(nb):
        qi = q[:, :, i * BLOCK:(i + 1) * BLOCK]
        ko, vo = kb[:, :, i], vb[:, :, i]
        q_sub = qi.reshape(B, H, nsub, QSUB, hd).transpose(2, 0, 1, 3, 4)
        if i == 0:
            def step(args, ko=ko, vo=vo):
                qc, bias_c = args
                logits = jnp.einsum('bhqd,bhkd->bhqk', qc, ko).astype(jnp.float32) * scale + bias_c[None]
                p = jax.nn.softmax(logits, axis=-1).astype(vo.dtype)
                return jnp.einsum('bhqk,bhkd->bhqd', p, vo)
            o = lax.map(step, (q_sub, own_bias_sub))
        else:
            ksel = min(TOPK, i)
            gate = jnp.einsum('bhqd,bhnd->bhqn', qi, kmean[:, :, :i]).astype(jnp.float32)
            _, idx = lax.top_k(gate, ksel)
            idx_sub = idx.reshape(B, H, nsub, QSUB, ksel).transpose(2, 0, 1, 3, 4)
            qpos_sub = (i * BLOCK + loc).reshape(nsub, QSUB)
            kp, vp = kb[:, :, :i], vb[:, :, :i]

            def step(args, ko=ko, vo=vo, kp=kp, vp=vp, ksel=ksel):
                qc, bias_c, ic, pc = args
                own = jnp.einsum('bhqd,bhkd->bhqk', qc, ko).astype(jnp.float32) * scale + bias_c[None]
                kg = _gather_blocks(kp, ic)
                vg = _gather_blocks(vp, ic)
                sel = jnp.einsum('bhqd,bhqnkd->bhqnk', qc, kg).astype(jnp.float32) * scale
                kpos = ic[..., None] * BLOCK + loc
                dist = pc[None, None, :, None, None] - kpos
                sel = sel + table_t[head_ix, _t5_bucket(dist)]
                logits = jnp.concatenate([own, sel.reshape(B, H, QSUB, ksel * BLOCK)], axis=-1)
                p = jax.nn.softmax(logits, axis=-1).astype(vo.dtype)
                p_sel = p[..., BLOCK:].reshape(B, H, QSUB, ksel, BLOCK)
                return (jnp.einsum('bhqk,bhkd->bhqd', p[..., :BLOCK], vo)
                        + jnp.einsum('bhqnk,bhqnkd->bhqd', p_sel, vg))
            o = lax.map(step, (q_sub, own_bias_sub, idx_sub, qpos_sub))
        outs.append(o.transpose(1, 2, 0, 3, 4).reshape(B, H, BLOCK, hd))
    return jnp.concatenate(outs, axis=2)


def _short_conv(h, b_gate, c_gate, conv_w):
    u = c_gate * h
    y = lax.conv_general_dilated(u, conv_w[:, None, :].astype(u.dtype), window_strides=(1,),
                                 padding=[(CONV_WIDTH - 1, 0)],
                                 dimension_numbers=('NWC', 'WIO', 'NWC'),
                                 feature_group_count=u.shape[-1])
    return b_gate * y


def _mixer(h, w_in, q_norm, k_norm, rel_bias, conv_w, attn_out_norm, conv_out_norm, w_out):
    B, S, _ = h.shape
    proj = h @ w_in
    q, k, v, cg, bg, hc = jnp.split(proj, [ATT_W, 2 * ATT_W, 3 * ATT_W,
                                           3 * ATT_W + CONV_W, 3 * ATT_W + 2 * CONV_W], axis=-1)
    q = _rmsnorm(q.reshape(B, S, N_ATT_HEADS, HEAD_DIM), q_norm).transpose(0, 2, 1, 3)
    k = _rmsnorm(k.reshape(B, S, N_ATT_HEADS, HEAD_DIM), k_norm).transpose(0, 2, 1, 3)
    v = v.reshape(B, S, N_ATT_HEADS, HEAD_DIM).transpose(0, 2, 1, 3)
    sp = -(-S // BLOCK) * BLOCK
    pad = ((0, 0), (0, 0), (0, sp - S), (0, 0))
    attn = _moba_attention(jnp.pad(q, pad), jnp.pad(k, pad), jnp.pad(v, pad), rel_bias)[:, :, :S]
    attn = attn.transpose(0, 2, 1, 3).reshape(B, S, ATT_W)
    conv = _short_conv(hc, bg, cg, conv_w)
    y = jnp.concatenate([_rmsnorm(attn, attn_out_norm), _rmsnorm(conv, conv_out_norm)], axis=-1)
    return y @ w_out


def setup_inputs(seed: int = 0) -> dict:
    key = jax.random.key(seed)
    ks = jax.random.split(key, 24)
    f32 = jnp.float32
    nrm = lambda k, shape, s: jax.random.normal(k, shape, f32) * s
    gain = lambda k, shape: 1.0 + 0.02 * jax.random.normal(k, shape, f32)
    L = DEPTH
    return {
        "x": nrm(ks[0], (BATCH, SEQ, D_MODEL), 1.0),
        "c": nrm(ks[1], (BATCH, D_MODEL), 1.0),
        "ada_w": nrm(ks[2], (L, D_MODEL, N_ADA * D_MODEL), 0.5 * D_MODEL ** -0.5),
        "ada_b": nrm(ks[3], (L, N_ADA * D_MODEL), 0.01),
        "ffn1_norm": gain(ks[4], (L, D_MODEL)),
        "ffn1_w_gate": nrm(ks[5], (L, D_MODEL, D_FF), D_MODEL ** -0.5),
        "ffn1_w_up": nrm(ks[6], (L, D_MODEL, D_FF), D_MODEL ** -0.5),
        "ffn1_w_down": nrm(ks[7], (L, D_FF, D_MODEL), D_FF ** -0.5),
        "mix_norm": gain(ks[8], (L, D_MODEL)),
        "w_in": nrm(ks[9], (L, D_MODEL, IN_COLS), D_MODEL ** -0.5),
        "q_norm": gain(ks[10], (L, HEAD_DIM)),
        "k_norm": gain(ks[11], (L, HEAD_DIM)),
        "rel_bias": nrm(ks[12], (N_BUCKETS, N_ATT_HEADS), 0.5),
        "conv_w": nrm(ks[13], (L, CONV_WIDTH, CONV_W), CONV_WIDTH ** -0.5),
        "attn_out_norm": gain(ks[14], (L, ATT_W)),
        "conv_out_norm": gain(ks[15], (L, CONV_W)),
        "w_out": nrm(ks[16], (L, MIX_W, D_MODEL), MIX_W ** -0.5),
        "ffn2_norm": gain(ks[17], (L, D_MODEL)),
        "ffn2_w_gate": nrm(ks[18], (L, D_MODEL, D_FF), D_MODEL ** -0.5),
        "ffn2_w_up": nrm(ks[19], (L, D_MODEL, D_FF), D_MODEL ** -0.5),
        "ffn2_w_down": nrm(ks[20], (L, D_FF, D_MODEL), D_FF ** -0.5),
    }


def reference(x, c, ada_w, ada_b, ffn1_norm, ffn1_w_gate, ffn1_w_up, ffn1_w_down,
              mix_norm, w_in, q_norm, k_norm, rel_bias, conv_w, attn_out_norm, conv_out_norm,
              w_out, ffn2_norm, ffn2_w_gate, ffn2_w_up, ffn2_w_down):
    cond = jax.nn.silu(c)
    for l in range(DEPTH):
        ada = (cond @ ada_w[l] + ada_b[l])[:, None, :]
        sh1, sc1, g1, sh2, sc2, g2, sh3, sc3, g3 = jnp.split(ada, N_ADA, axis=-1)
        h = _modulate(_rmsnorm(x, ffn1_norm[l]), sh1, sc1)
        x = x + 0.5 * g1 * _swiglu(h, ffn1_w_gate[l], ffn1_w_up[l], ffn1_w_down[l])
        h = _modulate(_rmsnorm(x, mix_norm[l]), sh2, sc2)
        x = x + g2 * _mixer(h, w_in[l], q_norm[l], k_norm[l], rel_bias, conv_w[l],
                            attn_out_norm[l], conv_out_norm[l], w_out[l])
        h = _modulate(_rmsnorm(x, ffn2_norm[l]), sh3, sc3)
        x = x + 0.5 * g3 * _swiglu(h, ffn2_w_gate[l], ffn2_w_up[l], ffn2_w_down[l])
    return x
```

```python
import functools
import math

import numpy as np
import jax
import jax.numpy as jnp
from jax import lax
from jax.experimental import pallas as pl
from jax.experimental.pallas import tpu as pltpu

HEAD_DIM = 128
BLOCK = 256
TOPK = 3
CONV_WIDTH = 3
N_BUCKETS = 32
MAX_DISTANCE = 2048
N_ADA = 9
EPS = 1e-6
NEG = -1e30

LANES = 128
SUBLANES = 8
VMEM_LIMIT_BYTES = 60000 * 1024

F32 = jnp.float32
BF16 = jnp.bfloat16


def _params(*sem):
    return pltpu.CompilerParams(dimension_semantics=sem, vmem_limit_bytes=VMEM_LIMIT_BYTES)


def _pick(n, pref):
    if n <= pref:
        return n
    t = pref
    while n % t:
        t //= 2
    return t


def _ada_kernel(c_ref, w_ref, b_ref, o_ref):
    c = c_ref[...]
    cond = (c * jax.nn.sigmoid(c)).astype(BF16)
    o_ref[...] = jnp.dot(cond, w_ref[...].astype(BF16), preferred_element_type=F32) + b_ref[...]


def _ada(c_pad, ada_w, ada_b):
    rows, d = c_pad.shape
    n = ada_w.shape[1]
    tn = _pick(n, 512)
    return pl.pallas_call(
        _ada_kernel,
        grid=(n // tn,),
        in_specs=[pl.BlockSpec((rows, d), lambda j: (0, 0)),
                  pl.BlockSpec((d, tn), lambda j: (0, j)),
                  pl.BlockSpec((1, tn), lambda j: (0, j))],
        out_specs=pl.BlockSpec((rows, tn), lambda j: (0, j)),
        out_shape=jax.ShapeDtypeStruct((rows, n), F32),
        compiler_params=_params("parallel"),
        name="ada_proj",
    )(c_pad, ada_w, ada_b)


def _norm_mod_kernel(x_ref, w_ref, ada_ref, o_ref, *, k):
    x = x_ref[...]
    ms = jnp.mean(x * x, axis=-1, keepdims=True)
    y = x * lax.rsqrt(ms + EPS) * w_ref[...]
    shift = ada_ref[3 * k:3 * k + 1, :]
    scale = ada_ref[3 * k + 1:3 * k + 2, :]
    o_ref[...] = (y * (1.0 + scale) + shift).astype(BF16)


def _norm_mod(x2d, w, ada, k, seq):
    m, d = x2d.shape
    tr = _pick(seq, 256)
    return pl.pallas_call(
        functools.partial(_norm_mod_kernel, k=k),
        grid=(m // tr,),
        in_specs=[pl.BlockSpec((tr, d), lambda i: (i, 0)),
                  pl.BlockSpec((1, d), lambda i: (0, 0)),
                  pl.BlockSpec((None, N_ADA, d), lambda i: (i * tr // seq, 0, 0))],
        out_specs=pl.BlockSpec((tr, d), lambda i: (i, 0)),
        out_shape=jax.ShapeDtypeStruct((m, d), BF16),
        compiler_params=_params("parallel"),
        name=f"norm_mod{k}",
    )(x2d, w, ada)


def _ffn_kernel(h_ref, wg_ref, wu_ref, wd_ref, x_ref, ada_ref, o_ref, *, k, n_xchunks, xc, nchunk):
    f = pl.program_id(1)
    d = o_ref.shape[1]

    @pl.when(f == 0)
    def _():
        o_ref[...] = jnp.zeros_like(o_ref)

    for c in range(n_xchunks):
        @pl.when(f == c)
        def _(c=c):
            o_ref[:, c * xc:(c + 1) * xc] += x_ref[...]

    h = h_ref[...]
    g = jnp.dot(h, wg_ref[...], preferred_element_type=F32)
    u = jnp.dot(h, wu_ref[...], preferred_element_type=F32)
    a = (g * jax.nn.sigmoid(g) * u).astype(BF16)
    half_gate = 0.5 * ada_ref[3 * k + 2:3 * k + 3, :]
    for n0 in range(0, d, nchunk):
        part = jnp.dot(a, wd_ref[:, n0:n0 + nchunk], preferred_element_type=F32)
        o_ref[:, n0:n0 + nchunk] += half_gate[:, n0:n0 + nchunk] * part


def _ffn(h, wg, wu, wd, x2d, ada, k, seq, tm_pref=512, tf_pref=512):
    m, d = h.shape
    fp = wg.shape[1]
    tm = _pick(seq, tm_pref)
    tf = _pick(fp, tf_pref)
    nf = fp // tf
    n_xchunks = 1
    while d % n_xchunks or (d // n_xchunks) % LANES or d // n_xchunks > 512:
        n_xchunks += 1
    assert n_xchunks <= nf, (n_xchunks, nf)
    xc = d // n_xchunks
    nchunk = _pick(d, 512)
    return pl.pallas_call(
        functools.partial(_ffn_kernel, k=k, n_xchunks=n_xchunks, xc=xc, nchunk=nchunk),
        grid=(m // tm, nf),
        in_specs=[pl.BlockSpec((tm, d), lambda i, f: (i, 0)),
                  pl.BlockSpec((d, tf), lambda i, f: (0, f)),
                  pl.BlockSpec((d, tf), lambda i, f: (0, f)),
                  pl.BlockSpec((tf, d), lambda i, f: (f, 0)),
                  pl.BlockSpec((tm, xc), lambda i, f: (i, jnp.minimum(f, n_xchunks - 1))),
                  pl.BlockSpec((None, N_ADA, d), lambda i, f: (i * tm // seq, 0, 0))],
        out_specs=pl.BlockSpec((tm, d), lambda i, f: (i, 0)),
        out_shape=jax.ShapeDtypeStruct((m, d), F32),
        compiler_params=_params("parallel", "arbitrary"),
        name=f"ffn{k}",
    )(h, wg, wu, wd, x2d, ada)


def _qkv_kernel(h_ref, w_ref, nw_ref, o_ref, *, heads_per_tile, tiles_per_sec):
    j = pl.program_id(1)
    res = jnp.dot(h_ref[...], w_ref[...], preferred_element_type=F32)
    sec = j // tiles_per_sec

    @pl.when(sec < 2)
    def _():
        w = nw_ref[pl.ds(sec, 1), :]
        for hh in range(heads_per_tile):
            blk = res[:, hh * HEAD_DIM:(hh + 1) * HEAD_DIM]
            ms = jnp.mean(blk * blk, axis=-1, keepdims=True)
            o_ref[hh] = (blk * lax.rsqrt(ms + EPS) * w).astype(BF16)

    @pl.when(sec >= 2)
    def _():
        for hh in range(heads_per_tile):
            o_ref[hh] = res[:, hh * HEAD_DIM:(hh + 1) * HEAD_DIM].astype(BF16)


def _qkv(h, w_qkv, qk_norm_w, seq, att_w):
    m, d = h.shape
    n = w_qkv.shape[1]
    tm = _pick(seq, 1024)
    tn = _pick(att_w, 1024)
    hpt = tn // HEAD_DIM
    return pl.pallas_call(
        functools.partial(_qkv_kernel, heads_per_tile=hpt, tiles_per_sec=att_w // tn),
        grid=(m // tm, n // tn),
        in_specs=[pl.BlockSpec((tm, d), lambda i, j: (i, 0)),
                  pl.BlockSpec((d, tn), lambda i, j: (0, j)),
                  pl.BlockSpec((SUBLANES, HEAD_DIM), lambda i, j: (0, 0))],
        out_specs=pl.BlockSpec((hpt, tm, HEAD_DIM), lambda i, j: (j, i, 0)),
        out_shape=jax.ShapeDtypeStruct((n // HEAD_DIM, m, HEAD_DIM), BF16),
        compiler_params=_params("parallel", "arbitrary"),
        name="qkv_proj",
    )(h, w_qkv, qk_norm_w)


def _conv_kernel(h_ref, wc_ref, wb_ref, wh_ref, cw_ref, o_ref, ubuf_ref, *, tiles_per_batch):
    i = pl.program_id(1)
    tm = h_ref.shape[0]

    @pl.when(i % tiles_per_batch == 0)
    def _():
        ubuf_ref[0:SUBLANES, :] = jnp.zeros((SUBLANES, ubuf_ref.shape[1]), F32)

    h = h_ref[...]
    cg = jnp.dot(h, wc_ref[...], preferred_element_type=F32)
    hc = jnp.dot(h, wh_ref[...], preferred_element_type=F32)
    ubuf_ref[SUBLANES:SUBLANES + tm, :] = cg * hc
    u0 = ubuf_ref[SUBLANES:SUBLANES + tm, :]
    u1 = ubuf_ref[SUBLANES - 1:SUBLANES - 1 + tm, :]
    u2 = ubuf_ref[SUBLANES - 2:SUBLANES - 2 + tm, :]
    y = cw_ref[0:1, :] * u2 + cw_ref[1:2, :] * u1 + cw_ref[2:3, :] * u0
    bg = jnp.dot(h, wb_ref[...], preferred_element_type=F32)
    o_ref[...] = bg * y
    ubuf_ref[0:SUBLANES, :] = ubuf_ref[tm:tm + SUBLANES, :]


def _conv(h, w_cg, w_bg, w_hc, conv_w, seq):
    m, d = h.shape
    cw = w_cg.shape[1]
    tm = _pick(seq, 1024)
    tc = _pick(cw, 256)
    wspec = pl.BlockSpec((d, tc), lambda c, i: (0, c))
    return pl.pallas_call(
        functools.partial(_conv_kernel, tiles_per_batch=seq // tm),
        grid=(cw // tc, m // tm),
        in_specs=[pl.BlockSpec((tm, d), lambda c, i: (i, 0)), wspec, wspec, wspec,
                  pl.BlockSpec((CONV_WIDTH, tc), lambda c, i: (0, c))],
        out_specs=pl.BlockSpec((tm, tc), lambda c, i: (i, c)),
        out_shape=jax.ShapeDtypeStruct((m, cw), F32),
        scratch_shapes=[pltpu.VMEM((tm + SUBLANES, tc), F32)],
        compiler_params=_params("arbitrary", "arbitrary"),
        name="conv_proj",
    )(h, w_cg, w_bg, w_hc, conv_w)


def _t5_bucket_np(dist):
    max_exact = N_BUCKETS // 2
    d = np.maximum(dist, 1).astype(np.float32)
    large = max_exact + (np.log(d / np.float32(max_exact)) / np.float32(math.log(MAX_DISTANCE / max_exact))
                         * np.float32(N_BUCKETS - max_exact)).astype(np.int32)
    large = np.minimum(large, N_BUCKETS - 1)
    return np.where(dist < max_exact, dist, large)


def _bias_plan(seq):
    buckets = _t5_bucket_np(np.arange(seq, dtype=np.int64))
    assert (np.diff(buckets) >= 0).all()
    thr = [int(np.argmax(buckets >= b)) if (buckets >= b).any() else seq for b in range(N_BUCKETS)]
    nb = seq // BLOCK
    n_tiles = nb
    for delta in range(1, nb):
        if buckets[delta * BLOCK - (BLOCK - 1)] == buckets[-1]:
            n_tiles = delta + 1
            break
    ranges = []
    for delta in range(n_tiles):
        lo = max(0, delta * BLOCK - (BLOCK - 1))
        hi = min(seq - 1, delta * BLOCK + (BLOCK - 1))
        ranges.append((int(buckets[lo]), int(buckets[hi])))
    return thr, ranges


def _attn_kernel(tbl_ref, q_ref, k_ref, v_ref, o_ref, bias_ref, kmean_ref, *, nb, thr, ranges):
    hd = pl.program_id(0)
    b = pl.program_id(1)
    scale = HEAD_DIM ** -0.5
    n_tiles = len(ranges)
    nt_dims = (((1,), (1,)), ((), ()))

    @pl.when(b == 0)
    def _build_bias():
        rel = (lax.broadcasted_iota(jnp.int32, (BLOCK, BLOCK), 0)
               - lax.broadcasted_iota(jnp.int32, (BLOCK, BLOCK), 1))
        for delta, (b_lo, b_hi) in enumerate(ranges):
            dist = rel + delta * BLOCK
            val = jnp.full((BLOCK, BLOCK), tbl_ref[hd, b_lo], F32)
            for bk in range(b_lo + 1, b_hi + 1):
                val = jnp.where(dist >= thr[bk], tbl_ref[hd, bk], val)
            if delta == 0:
                val = jnp.where(rel >= 0, val, NEG)
            bias_ref[delta] = val

    kmean_ref[...] = jnp.zeros_like(kmean_ref)
    for j in range(nb):
        kj = k_ref[j * BLOCK:(j + 1) * BLOCK, :].astype(F32)
        kmean_ref[j:j + 1, :] = jnp.sum(kj, axis=0, keepdims=True) * (1.0 / BLOCK)
    kmean = kmean_ref[...].astype(BF16)

    lane = lax.broadcasted_iota(jnp.int32, (BLOCK, LANES), 1)
    lane_row = lax.broadcasted_iota(jnp.int32, (1, LANES), 1)

    def softmax_tile(s, vj, m, l, acc):
        m_new = jnp.maximum(m, jnp.max(s, axis=-1, keepdims=True))
        alpha = jnp.exp(m - m_new)
        p = jnp.exp(s - m_new)
        l = alpha * l + jnp.sum(p, axis=-1, keepdims=True)
        acc = alpha * acc + jnp.dot(p.astype(BF16), vj, preferred_element_type=F32)
        return m_new, l, acc

    def q_block(i, carry):
        row0 = pl.multiple_of(i * BLOCK, BLOCK)
        qn = q_ref[pl.ds(row0, BLOCK), :]
        gate = lax.dot_general(qn, kmean, nt_dims, preferred_element_type=F32)
        rank = jnp.zeros((BLOCK, LANES), jnp.int32)
        for jp in range(nb):
            live = (jp < i).astype(jnp.int32)
            tie_row = jnp.where(lane_row > jp, live, 0)
            gcol = gate[:, jp:jp + 1]
            rank = rank + jnp.where(gcol > gate, live, jnp.where(gcol == gate, tie_row, 0))
        keep = jnp.where(lane < i, rank, TOPK) < TOPK
        qa = jnp.concatenate([qn, jnp.where(keep, 0.0, NEG).astype(BF16)], axis=1)

        ko = k_ref[pl.ds(row0, BLOCK), :]
        vo = v_ref[pl.ds(row0, BLOCK), :]
        s = lax.dot_general(qn, ko, nt_dims, preferred_element_type=F32) * scale + bias_ref[0]
        m = jnp.max(s, axis=-1, keepdims=True)
        p = jnp.exp(s - m)
        l = jnp.sum(p, axis=-1, keepdims=True)
        acc = jnp.dot(p.astype(BF16), vo, preferred_element_type=F32)

        def past_block(j, mla):
            r0 = pl.multiple_of(j * BLOCK, BLOCK)
            kj = k_ref[pl.ds(r0, BLOCK), :]
            vj = v_ref[pl.ds(r0, BLOCK), :]
            ka = jnp.concatenate([kj, jnp.where(lane == j, 1.0, 0.0).astype(BF16)], axis=1)
            t = jnp.minimum(i - j, n_tiles - 1)
            s = lax.dot_general(qa, ka, nt_dims, preferred_element_type=F32) * scale + bias_ref[t]
            return softmax_tile(s, vj, *mla)

        m, l, acc = lax.fori_loop(0, i, past_block, (m, l, acc))
        o_ref[pl.ds(row0, BLOCK), :] = acc / l
        return carry

    lax.fori_loop(0, nb, q_block, 0)


def _attention(qkv, rel_bias_t, batch, seq, n_heads):
    nb = seq // BLOCK
    thr, ranges = _bias_plan(seq)
    m = batch * seq
    head_spec = lambda off: pl.BlockSpec((None, seq, HEAD_DIM), lambda h, b: (off + h, b, 0))
    return pl.pallas_call(
        functools.partial(_attn_kernel, nb=nb, thr=thr, ranges=ranges),
        grid=(n_heads, batch),
        in_specs=[pl.BlockSpec(memory_space=pltpu.SMEM),
                  head_spec(0), head_spec(n_heads), head_spec(2 * n_heads)],
        out_specs=pl.BlockSpec((seq, HEAD_DIM), lambda h, b: (b, h)),
        out_shape=jax.ShapeDtypeStruct((m, n_heads * HEAD_DIM), F32),
        scratch_shapes=[pltpu.VMEM((len(ranges), BLOCK, BLOCK), F32),
                        pltpu.VMEM((LANES, HEAD_DIM), F32)],
        compiler_params=_params("arbitrary", "arbitrary"),
        name="moba_attn",
    )(rel_bias_t, qkv, qkv, qkv)


def _outproj_kernel(attn_ref, conv_ref, wa_ref, wc_ref, w_ref, x_ref, ada_ref, o_ref, y_ref, *, k):
    j = pl.program_id(1)
    aw = attn_ref.shape[1]

    @pl.when(j == 0)
    def _():
        a = attn_ref[...]
        y_ref[:, :aw] = (a * lax.rsqrt(jnp.mean(a * a, axis=-1, keepdims=True) + EPS) * wa_ref[...]).astype(BF16)
        c = conv_ref[...]
        y_ref[:, aw:] = (c * lax.rsqrt(jnp.mean(c * c, axis=-1, keepdims=True) + EPS) * wc_ref[...]).astype(BF16)

    res = jnp.dot(y_ref[...], w_ref[...], preferred_element_type=F32)
    o_ref[...] = x_ref[...] + ada_ref[3 * k + 2:3 * k + 3, :] * res


def _outproj(attn, conv, wa, wc, w_out, x2d, ada, k, seq):
    m, aw = attn.shape
    cw = conv.shape[1]
    d = w_out.shape[1]
    tm = _pick(seq, 512)
    tn = _pick(d, 1024)
    return pl.pallas_call(
        functools.partial(_outproj_kernel, k=k),
        grid=(m // tm, d // tn),
        in_specs=[pl.BlockSpec((tm, aw), lambda i, j: (i, 0)),
                  pl.BlockSpec((tm, cw), lambda i, j: (i, 0)),
                  pl.BlockSpec((1, aw), lambda i, j: (0, 0)),
                  pl.BlockSpec((1, cw), lambda i, j: (0, 0)),
                  pl.BlockSpec((aw + cw, tn), lambda i, j: (0, j)),
                  pl.BlockSpec((tm, tn), lambda i, j: (i, j)),
                  pl.BlockSpec((None, N_ADA, tn), lambda i, j: (i * tm // seq, 0, j))],
        out_specs=pl.BlockSpec((tm, tn), lambda i, j: (i, j)),
        out_shape=jax.ShapeDtypeStruct((m, d), F32),
        scratch_shapes=[pltpu.VMEM((tm, aw + cw), BF16)],
        compiler_params=_params("parallel", "arbitrary"),
        name="out_proj",
    )(attn, conv, wa, wc, w_out, x2d, ada)


def _pad_to(a, axis, mult):
    pad = -a.shape[axis] % mult
    if not pad:
        return a
    widths = [(0, 0)] * a.ndim
    widths[axis] = (0, pad)
    return jnp.pad(a, widths)


def _ffn_weights(w_gate, w_up, w_down, tf):
    return (_pad_to(w_gate.astype(BF16), 1, tf), _pad_to(w_up.astype(BF16), 1, tf),
            _pad_to(w_down.astype(BF16), 0, tf))


def kernel(x, c, ada_w, ada_b, ffn1_norm, ffn1_w_gate, ffn1_w_up, ffn1_w_down, mix_norm, w_in, q_norm, k_norm,
           rel_bias, conv_w, attn_out_norm, conv_out_norm, w_out, ffn2_norm, ffn2_w_gate, ffn2_w_up, ffn2_w_down):
    batch, seq, d = x.shape
    depth = ada_w.shape[0]
    mix_w = w_out.shape[1]
    att_w = mix_w // 2
    conv_cw = mix_w - att_w
    n_heads = att_w // HEAD_DIM
    assert seq % BLOCK == 0 and att_w % HEAD_DIM == 0
    ffn_tile = 512

    xs = x.reshape(batch * seq, d)
    c_pad = _pad_to(c, 0, SUBLANES)
    rel_bias_t = rel_bias.T
    for l in range(depth):
        ada = _ada(c_pad, ada_w[l], ada_b[l][None, :])[:batch].reshape(batch, N_ADA, d)

        h = _norm_mod(xs, ffn1_norm[l][None, :], ada, 0, seq)
        xs = _ffn(h, *_ffn_weights(ffn1_w_gate[l], ffn1_w_up[l], ffn1_w_down[l], ffn_tile), xs, ada, 0, seq)

        h = _norm_mod(xs, mix_norm[l][None, :], ada, 1, seq)
        w_in_l = w_in[l]
        qk_norm_w = _pad_to(jnp.stack([q_norm[l], k_norm[l]]), 0, SUBLANES)
        qkv = _qkv(h, w_in_l[:, :3 * att_w].astype(BF16), qk_norm_w, seq, att_w)
        c0 = 3 * att_w
        conv = _conv(h, w_in_l[:, c0:c0 + conv_cw].astype(BF16),
                     w_in_l[:, c0 + conv_cw:c0 + 2 * conv_cw].astype(BF16),
                     w_in_l[:, c0 + 2 * conv_cw:c0 + 3 * conv_cw].astype(BF16), conv_w[l], seq)
        attn = _attention(qkv, rel_bias_t, batch, seq, n_heads)
        xs = _outproj(attn, conv, attn_out_norm[l][None, :], conv_out_norm[l][None, :],
                      w_out[l].astype(BF16), xs, ada, 1, seq)

        h = _norm_mod(xs, ffn2_norm[l][None, :], ada, 2, seq)
        xs = _ffn(h, *_ffn_weights(ffn2_w_gate[l], ffn2_w_up[l], ffn2_w_down[l], ffn_tile), xs, ada, 2, seq)
    return xs.reshape(batch, seq, d)
```

```python
import functools
import math

import numpy as np
import jax
import jax.numpy as jnp
from jax import lax
from jax.experimental import pallas as pl
from jax.experimental.pallas import tpu as pltpu

HEAD_DIM = 128
BLOCK = 256
TOPK = 3
CONV_WIDTH = 3
N_BUCKETS = 32
MAX_DISTANCE = 2048
N_ADA = 9
EPS = 1e-6
NEG = -1e30
PAIR = 2 * BLOCK
LOG2E = 1.4426950408889634
QK_SCALE_LOG2 = HEAD_DIM ** -0.5 * LOG2E

LANES = 128
SUBLANES = 8
VMEM_LIMIT_BYTES = 60000 * 1024

F32 = jnp.float32
BF16 = jnp.bfloat16


def _params(*sem):
    return pltpu.CompilerParams(dimension_semantics=sem, vmem_limit_bytes=VMEM_LIMIT_BYTES)


def _pick(n, pref):
    if n <= pref:
        return n
    t = pref
    while n % t:
        t //= 2
    return t


def _ada_kernel(c_ref, w_ref, b_ref, o_ref):
    c = c_ref[...]
    cond = (c * jax.nn.sigmoid(c)).astype(BF16)
    o_ref[...] = jnp.dot(cond, w_ref[...].astype(BF16), preferred_element_type=F32) + b_ref[...]


def _ada(c_pad, ada_w, ada_b):
    rows, d = c_pad.shape
    n = ada_w.shape[1]
    tn = _pick(n, 512)
    return pl.pallas_call(
        _ada_kernel,
        grid=(n // tn,),
        in_specs=[pl.BlockSpec((rows, d), lambda j: (0, 0)),
                  pl.BlockSpec((d, tn), lambda j: (0, j)),
                  pl.BlockSpec((1, tn), lambda j: (0, j))],
        out_specs=pl.BlockSpec((rows, tn), lambda j: (0, j)),
        out_shape=jax.ShapeDtypeStruct((rows, n), F32),
        compiler_params=_params("parallel"),
        name="ada_proj",
    )(c_pad, ada_w, ada_b)


def _norm_mod_kernel(x_ref, w_ref, ada_ref, o_ref, *, k):
    x = x_ref[...]
    ms = jnp.mean(x * x, axis=-1, keepdims=True)
    y = x * lax.rsqrt(ms + EPS) * w_ref[...]
    shift = ada_ref[3 * k:3 * k + 1, :]
    scale = ada_ref[3 * k + 1:3 * k + 2, :]
    o_ref[...] = (y * (1.0 + scale) + shift).astype(BF16)


def _norm_mod(x2d, w, ada, k, seq):
    m, d = x2d.shape
    tr = _pick(seq, 256)
    return pl.pallas_call(
        functools.partial(_norm_mod_kernel, k=k),
        grid=(m // tr,),
        in_specs=[pl.BlockSpec((tr, d), lambda i: (i, 0)),
                  pl.BlockSpec((1, d), lambda i: (0, 0)),
                  pl.BlockSpec((None, N_ADA, d), lambda i: (i * tr // seq, 0, 0))],
        out_specs=pl.BlockSpec((tr, d), lambda i: (i, 0)),
        out_shape=jax.ShapeDtypeStruct((m, d), BF16),
        compiler_params=_params("parallel"),
        name=f"norm_mod{k}",
    )(x2d, w, ada)


def _ffn_kernel(h_ref, wg_ref, wu_ref, wd_ref, x_ref, ada_ref, o_ref, *, k, n_xchunks, xc, nchunk):
    f = pl.program_id(1)
    d = o_ref.shape[1]

    @pl.when(f == 0)
    def _():
        o_ref[...] = jnp.zeros_like(o_ref)

    for c in range(n_xchunks):
        @pl.when(f == c)
        def _(c=c):
            o_ref[:, c * xc:(c + 1) * xc] += x_ref[...]

    h = h_ref[...]
    g = jnp.dot(h, wg_ref[...], preferred_element_type=F32)
    u = jnp.dot(h, wu_ref[...], preferred_element_type=F32)
    a = (g * jax.nn.sigmoid(g) * u).astype(BF16)
    half_gate = 0.5 * ada_ref[3 * k + 2:3 * k + 3, :]
    for n0 in range(0, d, nchunk):
        part = jnp.dot(a, wd_ref[:, n0:n0 + nchunk], preferred_element_type=F32)
        o_ref[:, n0:n0 + nchunk] += half_gate[:, n0:n0 + nchunk] * part


def _ffn(h, wg, wu, wd, x2d, ada, k, seq, tm_pref=512, tf_pref=512):
    m, d = h.shape
    fp = wg.shape[1]
    tm = _pick(seq, tm_pref)
    tf = _pick(fp, tf_pref)
    nf = fp // tf
    n_xchunks = 1
    while d % n_xchunks or (d // n_xchunks) % LANES or d // n_xchunks > 512:
        n_xchunks += 1
    assert n_xchunks <= nf, (n_xchunks, nf)
    xc = d // n_xchunks
    nchunk = _pick(d, 512)
    return pl.pallas_call(
        functools.partial(_ffn_kernel, k=k, n_xchunks=n_xchunks, xc=xc, nchunk=nchunk),
        grid=(m // tm, nf),
        in_specs=[pl.BlockSpec((tm, d), lambda i, f: (i, 0)),
                  pl.BlockSpec((d, tf), lambda i, f: (0, f)),
                  pl.BlockSpec((d, tf), lambda i, f: (0, f)),
                  pl.BlockSpec((tf, d), lambda i, f: (f, 0)),
                  pl.BlockSpec((tm, xc), lambda i, f: (i, jnp.minimum(f, n_xchunks - 1))),
                  pl.BlockSpec((None, N_ADA, d), lambda i, f: (i * tm // seq, 0, 0))],
        out_specs=pl.BlockSpec((tm, d), lambda i, f: (i, 0)),
        out_shape=jax.ShapeDtypeStruct((m, d), F32),
        compiler_params=_params("parallel", "arbitrary"),
        name=f"ffn{k}",
    )(h, wg, wu, wd, x2d, ada)


def _head_norm(blk, w):
    ms = jnp.mean(blk * blk, axis=-1, keepdims=True)
    return blk * lax.rsqrt(ms + EPS) * w


def _k_kernel(h_ref, w_ref, nw_ref, o_ref, *, heads_per_tile):
    res = jnp.dot(h_ref[...], w_ref[...], preferred_element_type=F32)
    for hh in range(heads_per_tile):
        o_ref[hh] = _head_norm(res[:, hh * HEAD_DIM:(hh + 1) * HEAD_DIM], nw_ref[...]).astype(BF16)


def _store_transposed(blk, o_ref, hh):
    blk_t = blk.T.astype(BF16)
    for pp in range(o_ref.shape[1]):
        o_ref[hh, pp] = blk_t[:, pp * PAIR:(pp + 1) * PAIR]


def _qt_kernel(h_ref, w_ref, nw_ref, o_ref, *, heads_per_tile):
    res = jnp.dot(h_ref[...], w_ref[...], preferred_element_type=F32)
    for hh in range(heads_per_tile):
        blk = _head_norm(res[:, hh * HEAD_DIM:(hh + 1) * HEAD_DIM], nw_ref[...])
        _store_transposed(blk * QK_SCALE_LOG2, o_ref, hh)


def _vt_kernel(h_ref, w_ref, o_ref, *, heads_per_tile):
    res = jnp.dot(h_ref[...], w_ref[...], preferred_element_type=F32)
    for hh in range(heads_per_tile):
        _store_transposed(res[:, hh * HEAD_DIM:(hh + 1) * HEAD_DIM], o_ref, hh)


def _head_proj(h, w, norm_w, seq, kind):
    m, d = h.shape
    n = w.shape[1]
    tm = _pick(seq, 1024)
    tn = _pick(n, 1024)
    hpt = tn // HEAD_DIM
    in_specs = [pl.BlockSpec((tm, d), lambda i, j: (i, 0)),
                pl.BlockSpec((d, tn), lambda i, j: (0, j))]
    args = [h, w]
    if kind != "v":
        in_specs.append(pl.BlockSpec((1, HEAD_DIM), lambda i, j: (0, 0)))
        args.append(norm_w)
    if kind == "k":
        out_spec = pl.BlockSpec((hpt, tm, HEAD_DIM), lambda i, j: (j, i, 0))
        out_shape = (n // HEAD_DIM, m, HEAD_DIM)
    else:
        out_spec = pl.BlockSpec((hpt, tm // PAIR, HEAD_DIM, PAIR), lambda i, j: (j, i, 0, 0))
        out_shape = (n // HEAD_DIM, m // PAIR, HEAD_DIM, PAIR)
    body = {"q": _qt_kernel, "k": _k_kernel, "v": _vt_kernel}[kind]
    return pl.pallas_call(
        functools.partial(body, heads_per_tile=hpt),
        grid=(m // tm, n // tn),
        in_specs=in_specs,
        out_specs=out_spec,
        out_shape=jax.ShapeDtypeStruct(out_shape, BF16),
        compiler_params=_params("parallel", "arbitrary"),
        name=f"{kind}_proj",
    )(*args)


def _conv_kernel(h_ref, wc_ref, wb_ref, wh_ref, cw_ref, o_ref, ubuf_ref, *, tiles_per_batch):
    i = pl.program_id(1)
    tm = h_ref.shape[0]

    @pl.when(i % tiles_per_batch == 0)
    def _():
        ubuf_ref[0:SUBLANES, :] = jnp.zeros((SUBLANES, ubuf_ref.shape[1]), F32)

    h = h_ref[...]
    cg = jnp.dot(h, wc_ref[...], preferred_element_type=F32)
    hc = jnp.dot(h, wh_ref[...], preferred_element_type=F32)
    ubuf_ref[SUBLANES:SUBLANES + tm, :] = cg * hc
    u0 = ubuf_ref[SUBLANES:SUBLANES + tm, :]
    u1 = ubuf_ref[SUBLANES - 1:SUBLANES - 1 + tm, :]
    u2 = ubuf_ref[SUBLANES - 2:SUBLANES - 2 + tm, :]
    y = cw_ref[0:1, :] * u2 + cw_ref[1:2, :] * u1 + cw_ref[2:3, :] * u0
    bg = jnp.dot(h, wb_ref[...], preferred_element_type=F32)
    o_ref[...] = bg * y
    ubuf_ref[0:SUBLANES, :] = ubuf_ref[tm:tm + SUBLANES, :]


def _conv(h, w_cg, w_bg, w_hc, conv_w, seq):
    m, d = h.shape
    cw = w_cg.shape[1]
    tm = _pick(seq, 1024)
    tc = _pick(cw, 256)
    wspec = pl.BlockSpec((d, tc), lambda c, i: (0, c))
    return pl.pallas_call(
        functools.partial(_conv_kernel, tiles_per_batch=seq // tm),
        grid=(cw // tc, m // tm),
        in_specs=[pl.BlockSpec((tm, d), lambda c, i: (i, 0)), wspec, wspec, wspec,
                  pl.BlockSpec((CONV_WIDTH, tc), lambda c, i: (0, c))],
        out_specs=pl.BlockSpec((tm, tc), lambda c, i: (i, c)),
        out_shape=jax.ShapeDtypeStruct((m, cw), F32),
        scratch_shapes=[pltpu.VMEM((tm + SUBLANES, tc), F32)],
        compiler_params=_params("arbitrary", "arbitrary"),
        name="conv_proj",
    )(h, w_cg, w_bg, w_hc, conv_w)


def _t5_bucket_np(dist):
    max_exact = N_BUCKETS // 2
    d = np.maximum(dist, 1).astype(np.float32)
    large = max_exact + (np.log(d / np.float32(max_exact)) / np.float32(math.log(MAX_DISTANCE / max_exact))
                         * np.float32(N_BUCKETS - max_exact)).astype(np.int32)
    large = np.minimum(large, N_BUCKETS - 1)
    return np.where(dist < max_exact, dist, large)


def _bias_plan(seq):
    buckets = _t5_bucket_np(np.arange(seq, dtype=np.int64))
    assert (np.diff(buckets) >= 0).all()
    thr = [int(np.argmax(buckets >= b)) if (buckets >= b).any() else seq for b in range(N_BUCKETS)]
    nb = seq // BLOCK
    n_tiles = nb
    for delta in range(1, nb):
        if buckets[delta * BLOCK - (BLOCK - 1)] == buckets[-1]:
            n_tiles = delta + 1
            break
    ranges = []
    for delta in range(n_tiles):
        lo = max(0, delta * BLOCK - (BLOCK - 1))
        hi = min(seq - 1, delta * BLOCK + (BLOCK - 1))
        ranges.append((int(buckets[lo]), int(buckets[hi])))
    return thr, ranges


def _work_items(nb):
    items = [(u, jj) for u in range(nb // 2) for jj in range(u + 1)]
    n = len(items)
    n_steps = n + 1 + (n + 1) % 2
    tab = np.zeros((3, n_steps), np.int32)
    for t in range(n_steps):
        tab[0, t], tab[1, t] = items[min(t, n - 1)]
        tab[2, t] = 1 if 1 <= t <= n and items[t - 1][1] == items[t - 1][0] else 0
    return tab


def _attn_kernel(tbl_ref, item_ref, qt_ref, k_ref, vt_ref, o_ref, bias_ref, kmean_ref, qa_ref,
                 logit0_ref, logit1_ref, *, nb, thr, ranges, n_steps):
    hd = pl.program_id(0)
    b = pl.program_id(1)
    n_tiles = len(ranges)
    n_sb = nb // 2

    @pl.when(b == 0)
    def _build_bias():
        rel = (lax.broadcasted_iota(jnp.int32, (BLOCK, BLOCK), 1)
               - lax.broadcasted_iota(jnp.int32, (BLOCK, BLOCK), 0))
        for delta, (b_lo, b_hi) in enumerate(ranges):
            dist = rel + delta * BLOCK
            val = jnp.full((BLOCK, BLOCK), tbl_ref[hd, b_lo], F32)
            for bk in range(b_lo + 1, b_hi + 1):
                val = jnp.where(dist >= thr[bk], tbl_ref[hd, bk], val)
            val = val * LOG2E
            if delta == 0:
                val = jnp.where(rel >= 0, val, NEG)
            bias_ref[delta] = val

    for j in range(nb):
        kj = k_ref[j * BLOCK:(j + 1) * BLOCK, :].astype(F32)
        kmean_ref[j:j + 1, :] = jnp.sum(kj, axis=0, keepdims=True) * (1.0 / BLOCK)
    kmean = kmean_ref[...].astype(BF16)

    blk_id = lax.broadcasted_iota(jnp.int32, (nb, PAIR), 0)
    q_half = (lax.broadcasted_iota(jnp.int32, (1, PAIR), 1) >= BLOCK).astype(jnp.int32)

    def build_qa(u, carry):
        q_t = qt_ref[u]
        q_blk = 2 * u + q_half
        gate = jnp.dot(kmean, q_t, preferred_element_type=F32)
        rank = jnp.zeros((nb, PAIR), jnp.int32)
        for jp in range(nb):
            live = (q_blk > jp).astype(jnp.int32)
            tie = jnp.where(blk_id > jp, live, 0)
            g_jp = gate[jp:jp + 1, :]
            rank = rank + jnp.where(g_jp > gate, live, jnp.where(g_jp == gate, tie, 0))
        keep = jnp.where(blk_id < q_blk, rank, jnp.where(blk_id == q_blk, 0, TOPK)) < TOPK
        qa_ref[u, :HEAD_DIM] = q_t
        qa_ref[u, HEAD_DIM:HEAD_DIM + nb] = jnp.where(keep, 0.0, NEG).astype(BF16)
        return carry

    qa_ref[:, HEAD_DIM + nb:] = jnp.zeros((n_sb, LANES - nb, PAIR), BF16)
    lax.fori_loop(0, n_sb, build_qa, 0)

    k_half = (lax.broadcasted_iota(jnp.int32, (PAIR, LANES), 0) >= BLOCK).astype(jnp.int32)
    lane = lax.broadcasted_iota(jnp.int32, (PAIR, LANES), 1)

    def qk_logits(t, logit_ref):
        u = item_ref[0, t]
        jj = item_ref[1, t]
        r0 = pl.multiple_of(jj * PAIR, PAIR)
        onehot = jnp.where(lane == 2 * jj + k_half, 1.0, 0.0).astype(BF16)
        ka = jnp.concatenate([k_ref[pl.ds(r0, PAIR), :], onehot], axis=1)
        raw = jnp.dot(ka, qa_ref[u], preferred_element_type=F32)
        d0 = 2 * (u - jj)
        tile = lambda d: bias_ref[jnp.clip(d, 0, n_tiles - 1)]
        top = raw[:BLOCK] + jnp.concatenate([tile(d0), tile(d0 + 1)], axis=1)
        bot = raw[BLOCK:] + jnp.concatenate([tile(d0 - 1), tile(d0)], axis=1)
        logit_ref[:BLOCK] = top
        logit_ref[BLOCK:] = bot
        return jnp.maximum(jnp.max(top, axis=0, keepdims=True), jnp.max(bot, axis=0, keepdims=True))

    def step(t, c, logit_cur, logit_next):
        cmax, p_prev, alpha_prev, m, l, acc = c
        cmax_next = qk_logits(jnp.minimum(t + 1, n_steps - 1), logit_next)

        t_prev = jnp.maximum(t - 1, 0)
        acc = alpha_prev * acc + jnp.dot(vt_ref[item_ref[1, t_prev]], p_prev, preferred_element_type=F32)

        @pl.when(item_ref[2, t] == 1)
        def _():
            r0 = pl.multiple_of(item_ref[0, t_prev] * PAIR, PAIR)
            o_ref[pl.ds(r0, PAIR), :] = (acc * (1.0 / l)).T

        m_old = jnp.where(item_ref[1, t] == 0, NEG, m)
        m_new = jnp.maximum(m_old, cmax)
        alpha = jnp.exp2(m_old - m_new)
        p = jnp.exp2(logit_cur[...] - m_new)
        l_new = alpha * l + jnp.sum(p, axis=0, keepdims=True)
        return cmax_next, p.astype(BF16), alpha, m_new, l_new, acc

    def two_steps(tt, c):
        c = step(2 * tt, c, logit0_ref, logit1_ref)
        return step(2 * tt + 1, c, logit1_ref, logit0_ref)

    init = (qk_logits(0, logit0_ref), jnp.zeros((PAIR, PAIR), BF16), jnp.zeros((1, PAIR), F32),
            jnp.full((1, PAIR), NEG, F32), jnp.ones((1, PAIR), F32), jnp.zeros((HEAD_DIM, PAIR), F32))
    lax.fori_loop(0, n_steps // 2, two_steps, init)


def _attention(qt, k, vt, rel_bias_t, batch, seq, n_heads):
    nb = seq // BLOCK
    assert nb % 2 == 0 and nb <= LANES
    thr, ranges = _bias_plan(seq)
    items = _work_items(nb)
    n_steps = items.shape[1]
    m = batch * seq
    t_spec = pl.BlockSpec((None, seq // PAIR, HEAD_DIM, PAIR), lambda h, b: (h, b, 0, 0))
    return pl.pallas_call(
        functools.partial(_attn_kernel, nb=nb, thr=thr, ranges=ranges, n_steps=n_steps),
        grid=(n_heads, batch),
        in_specs=[pl.BlockSpec(memory_space=pltpu.SMEM),
                  pl.BlockSpec(memory_space=pltpu.SMEM),
                  t_spec,
                  pl.BlockSpec((None, seq, HEAD_DIM), lambda h, b: (h, b, 0)),
                  t_spec],
        out_specs=pl.BlockSpec((seq, HEAD_DIM), lambda h, b: (b, h)),
        out_shape=jax.ShapeDtypeStruct((m, n_heads * HEAD_DIM), F32),
        scratch_shapes=[pltpu.VMEM((len(ranges), BLOCK, BLOCK), F32),
                        pltpu.VMEM((nb, HEAD_DIM), F32),
                        pltpu.VMEM((nb // 2, 2 * LANES, PAIR), BF16),
                        pltpu.VMEM((PAIR, PAIR), F32),
                        pltpu.VMEM((PAIR, PAIR), F32)],
        compiler_params=_params("arbitrary", "arbitrary"),
        name="moba_attn",
    )(rel_bias_t, jnp.asarray(items), qt, k, vt)


def _outproj_kernel(attn_ref, conv_ref, wa_ref, wc_ref, w_ref, x_ref, ada_ref, o_ref, y_ref, *, k):
    j = pl.program_id(1)
    aw = attn_ref.shape[1]

    @pl.when(j == 0)
    def _():
        a = attn_ref[...]
        y_ref[:, :aw] = (a * lax.rsqrt(jnp.mean(a * a, axis=-1, keepdims=True) + EPS) * wa_ref[...]).astype(BF16)
        c = conv_ref[...]
        y_ref[:, aw:] = (c * lax.rsqrt(jnp.mean(c * c, axis=-1, keepdims=True) + EPS) * wc_ref[...]).astype(BF16)

    res = jnp.dot(y_ref[...], w_ref[...], preferred_element_type=F32)
    o_ref[...] = x_ref[...] + ada_ref[3 * k + 2:3 * k + 3, :] * res


def _outproj(attn, conv, wa, wc, w_out, x2d, ada, k, seq):
    m, aw = attn.shape
    cw = conv.shape[1]
    d = w_out.shape[1]
    tm = _pick(seq, 512)
    tn = _pick(d, 1024)
    return pl.pallas_call(
        functools.partial(_outproj_kernel, k=k),
        grid=(m // tm, d // tn),
        in_specs=[pl.BlockSpec((tm, aw), lambda i, j: (i, 0)),
                  pl.BlockSpec((tm, cw), lambda i, j: (i, 0)),
                  pl.BlockSpec((1, aw), lambda i, j: (0, 0)),
                  pl.BlockSpec((1, cw), lambda i, j: (0, 0)),
                  pl.BlockSpec((aw + cw, tn), lambda i, j: (0, j)),
                  pl.BlockSpec((tm, tn), lambda i, j: (i, j)),
                  pl.BlockSpec((None, N_ADA, tn), lambda i, j: (i * tm // seq, 0, j))],
        out_specs=pl.BlockSpec((tm, tn), lambda i, j: (i, j)),
        out_shape=jax.ShapeDtypeStruct((m, d), F32),
        scratch_shapes=[pltpu.VMEM((tm, aw + cw), BF16)],
        compiler_params=_params("parallel", "arbitrary"),
        name="out_proj",
    )(attn, conv, wa, wc, w_out, x2d, ada)


def _pad_to(a, axis, mult):
    pad = -a.shape[axis] % mult
    if not pad:
        return a
    widths = [(0, 0)] * a.ndim
    widths[axis] = (0, pad)
    return jnp.pad(a, widths)


def _ffn_weights(w_gate, w_up, w_down, tf):
    return (_pad_to(w_gate.astype(BF16), 1, tf), _pad_to(w_up.astype(BF16), 1, tf),
            _pad_to(w_down.astype(BF16), 0, tf))


def kernel(x, c, ada_w, ada_b, ffn1_norm, ffn1_w_gate, ffn1_w_up, ffn1_w_down, mix_norm, w_in, q_norm, k_norm,
           rel_bias, conv_w, attn_out_norm, conv_out_norm, w_out, ffn2_norm, ffn2_w_gate, ffn2_w_up, ffn2_w_down):
    batch, seq, d = x.shape
    depth = ada_w.shape[0]
    mix_w = w_out.shape[1]
    att_w = mix_w // 2
    conv_cw = mix_w - att_w
    n_heads = att_w // HEAD_DIM
    assert seq % BLOCK == 0 and att_w % HEAD_DIM == 0
    ffn_tile = 512

    xs = x.reshape(batch * seq, d)
    c_pad = _pad_to(c, 0, SUBLANES)
    rel_bias_t = rel_bias.T
    for l in range(depth):
        ada = _ada(c_pad, ada_w[l], ada_b[l][None, :])[:batch].reshape(batch, N_ADA, d)

        h = _norm_mod(xs, ffn1_norm[l][None, :], ada, 0, seq)
        xs = _ffn(h, *_ffn_weights(ffn1_w_gate[l], ffn1_w_up[l], ffn1_w_down[l], ffn_tile), xs, ada, 0, seq)

        h = _norm_mod(xs, mix_norm[l][None, :], ada, 1, seq)
        w_in_l = w_in[l]
        qt = _head_proj(h, w_in_l[:, :att_w].astype(BF16), q_norm[l][None, :], seq, "q")
        kk = _head_proj(h, w_in_l[:, att_w:2 * att_w].astype(BF16), k_norm[l][None, :], seq, "k")
        vt = _head_proj(h, w_in_l[:, 2 * att_w:3 * att_w].astype(BF16), None, seq, "v")
        c0 = 3 * att_w
        conv = _conv(h, w_in_l[:, c0:c0 + conv_cw].astype(BF16),
                     w_in_l[:, c0 + conv_cw:c0 + 2 * conv_cw].astype(BF16),
                     w_in_l[:, c0 + 2 * conv_cw:c0 + 3 * conv_cw].astype(BF16), conv_w[l], seq)
        attn = _attention(qt, kk, vt, rel_bias_t, batch, seq, n_heads)
        xs = _outproj(attn, conv, attn_out_norm[l][None, :], conv_out_norm[l][None, :],
                      w_out[l].astype(BF16), xs, ada, 1, seq)

        h = _norm_mod(xs, ffn2_norm[l][None, :], ada, 2, seq)
        xs = _ffn(h, *_ffn_weights(ffn2_w_gate[l], ffn2_w_up[l], ffn2_w_down[l], ffn_tile), xs, ada, 2, seq)
    return xs.reshape(batch, seq, d)
```

```python
import functools
import math

import numpy as np
import jax
import jax.numpy as jnp
from jax import lax
from jax.experimental import pallas as pl
from jax.experimental.pallas import tpu as pltpu

HEAD_DIM = 128
BLOCK = 256
TOPK = 3
CONV_WIDTH = 3
N_BUCKETS = 32
MAX_DISTANCE = 2048
N_ADA = 9
EPS = 1e-6
NEG = -1e30
PAIR = 2 * BLOCK
ATTN_UNROLL = 2
V_ROWS = HEAD_DIM + 16
LOG2E = 1.4426950408889634
QK_SCALE_LOG2 = HEAD_DIM ** -0.5 * LOG2E

LANES = 128
SUBLANES = 8
VMEM_LIMIT_BYTES = 60000 * 1024

F32 = jnp.float32
BF16 = jnp.bfloat16


def _params(*sem):
    return pltpu.CompilerParams(dimension_semantics=sem, vmem_limit_bytes=VMEM_LIMIT_BYTES)


def _pick(n, pref):
    if n <= pref:
        return n
    t = pref
    while n % t:
        t //= 2
    return t


def _ada_kernel(c_ref, w_ref, b_ref, o_ref):
    c = c_ref[...]
    cond = (c * jax.nn.sigmoid(c)).astype(BF16)
    o_ref[...] = jnp.dot(cond, w_ref[...].astype(BF16), preferred_element_type=F32) + b_ref[...]


def _ada(c_pad, ada_w, ada_b):
    rows, d = c_pad.shape
    n = ada_w.shape[1]
    tn = _pick(n, 512)
    return pl.pallas_call(
        _ada_kernel,
        grid=(n // tn,),
        in_specs=[pl.BlockSpec((rows, d), lambda j: (0, 0)),
                  pl.BlockSpec((d, tn), lambda j: (0, j)),
                  pl.BlockSpec((1, tn), lambda j: (0, j))],
        out_specs=pl.BlockSpec((rows, tn), lambda j: (0, j)),
        out_shape=jax.ShapeDtypeStruct((rows, n), F32),
        compiler_params=_params("parallel"),
        name="ada_proj",
    )(c_pad, ada_w, ada_b)


def _norm_mod_kernel(x_ref, w_ref, ada_ref, o_ref, *, k):
    x = x_ref[...]
    ms = jnp.mean(x * x, axis=-1, keepdims=True)
    y = x * lax.rsqrt(ms + EPS) * w_ref[...]
    shift = ada_ref[3 * k:3 * k + 1, :]
    scale = ada_ref[3 * k + 1:3 * k + 2, :]
    o_ref[...] = (y * (1.0 + scale) + shift).astype(BF16)


def _norm_mod(x2d, w, ada, k, seq):
    m, d = x2d.shape
    tr = _pick(seq, 512)
    return pl.pallas_call(
        functools.partial(_norm_mod_kernel, k=k),
        grid=(m // tr,),
        in_specs=[pl.BlockSpec((tr, d), lambda i: (i, 0)),
                  pl.BlockSpec((1, d), lambda i: (0, 0)),
                  pl.BlockSpec((None, N_ADA, d), lambda i: (i * tr // seq, 0, 0))],
        out_specs=pl.BlockSpec((tr, d), lambda i: (i, 0)),
        out_shape=jax.ShapeDtypeStruct((m, d), BF16),
        compiler_params=_params("parallel"),
        name=f"norm_mod{k}",
    )(x2d, w, ada)


def _ffn_kernel(h_ref, wg_ref, wu_ref, wd_ref, x_ref, ada_ref, o_ref, *, k, n_xchunks, xc, nchunk, tail):
    f = pl.program_id(1)
    nf = pl.num_programs(1)
    d = o_ref.shape[1]
    tf = wg_ref.shape[1]

    @pl.when(f == 0)
    def _():
        o_ref[...] = jnp.zeros_like(o_ref)

    for c in range(n_xchunks):
        @pl.when(f == c)
        def _(c=c):
            o_ref[:, c * xc:(c + 1) * xc] += x_ref[...]

    def accumulate(width):
        h = h_ref[...]
        g = jnp.dot(h, wg_ref[:, :width], preferred_element_type=F32)
        u = jnp.dot(h, wu_ref[:, :width], preferred_element_type=F32)
        a = (g * jax.nn.sigmoid(g) * u).astype(BF16)
        half_gate = 0.5 * ada_ref[3 * k + 2:3 * k + 3, :]
        for n0 in range(0, d, nchunk):
            part = jnp.dot(a, wd_ref[:width, n0:n0 + nchunk], preferred_element_type=F32)
            o_ref[:, n0:n0 + nchunk] += half_gate[:, n0:n0 + nchunk] * part

    if tail == tf:
        accumulate(tf)
    else:
        pl.when(f < nf - 1)(lambda: accumulate(tf))
        pl.when(f == nf - 1)(lambda: accumulate(tail))


def _ffn(h, wg, wu, wd, x2d, ada, k, seq, tm_pref=512, tf_pref=512):
    m, d = h.shape
    d_ff = wg.shape[1]
    tm = _pick(seq, tm_pref)
    tf = min(tf_pref, d_ff)
    nf = pl.cdiv(d_ff, tf)
    tail = d_ff - (nf - 1) * tf
    assert tail % LANES == 0, (d_ff, tf)
    n_xchunks = 1
    while d % n_xchunks or (d // n_xchunks) % LANES or d // n_xchunks > 512:
        n_xchunks += 1
    assert n_xchunks <= nf, (n_xchunks, nf)
    xc = d // n_xchunks
    nchunk = _pick(d, 512)
    return pl.pallas_call(
        functools.partial(_ffn_kernel, k=k, n_xchunks=n_xchunks, xc=xc, nchunk=nchunk, tail=tail),
        grid=(m // tm, nf),
        in_specs=[pl.BlockSpec((tm, d), lambda i, f: (i, 0)),
                  pl.BlockSpec((d, tf), lambda i, f: (0, f)),
                  pl.BlockSpec((d, tf), lambda i, f: (0, f)),
                  pl.BlockSpec((tf, d), lambda i, f: (f, 0)),
                  pl.BlockSpec((tm, xc), lambda i, f: (i, jnp.minimum(f, n_xchunks - 1))),
                  pl.BlockSpec((None, N_ADA, d), lambda i, f: (i * tm // seq, 0, 0))],
        out_specs=pl.BlockSpec((tm, d), lambda i, f: (i, 0)),
        out_shape=jax.ShapeDtypeStruct((m, d), F32),
        compiler_params=_params("parallel", "arbitrary"),
        name=f"ffn{k}",
    )(h, wg, wu, wd, x2d, ada)


def _head_norm(blk, w):
    ms = jnp.mean(blk * blk, axis=-1, keepdims=True)
    return blk * lax.rsqrt(ms + EPS) * w


def _k_kernel(h_ref, w_ref, nw_ref, o_ref, *, heads_per_tile, seq):
    tm = h_ref.shape[0]
    res = jnp.dot(h_ref[...], w_ref[...], preferred_element_type=F32)
    pos = (pl.program_id(0) * tm) % seq + lax.broadcasted_iota(jnp.int32, (tm, LANES), 0)
    lane = lax.broadcasted_iota(jnp.int32, (tm, LANES), 1)
    onehot = jnp.where(lane * BLOCK == pos - pos % BLOCK, 1.0, 0.0).astype(BF16)
    for hh in range(heads_per_tile):
        o_ref[hh, :, :HEAD_DIM] = _head_norm(res[:, hh * HEAD_DIM:(hh + 1) * HEAD_DIM], nw_ref[...]).astype(BF16)
        o_ref[hh, :, HEAD_DIM:] = onehot


def _store_transposed(blk, o_ref, hh):
    blk_t = blk.T.astype(BF16)
    for pp in range(o_ref.shape[1]):
        o_ref[hh, pp, :HEAD_DIM] = blk_t[:, pp * PAIR:(pp + 1) * PAIR]


def _qt_kernel(h_ref, w_ref, nw_ref, o_ref, *, heads_per_tile):
    res = jnp.dot(h_ref[...], w_ref[...], preferred_element_type=F32)
    for hh in range(heads_per_tile):
        blk = _head_norm(res[:, hh * HEAD_DIM:(hh + 1) * HEAD_DIM], nw_ref[...])
        _store_transposed(blk * QK_SCALE_LOG2, o_ref, hh)


def _vt_kernel(h_ref, w_ref, o_ref, *, heads_per_tile):
    res = jnp.dot(h_ref[...], w_ref[...], preferred_element_type=F32)
    for hh in range(heads_per_tile):
        _store_transposed(res[:, hh * HEAD_DIM:(hh + 1) * HEAD_DIM], o_ref, hh)
        o_ref[hh, :, HEAD_DIM:] = jnp.ones((o_ref.shape[1], V_ROWS - HEAD_DIM, PAIR), BF16)


def _head_proj(h, w, norm_w, seq, kind):
    m, d = h.shape
    n = w.shape[1]
    tm = _pick(seq, 1024)
    tn = _pick(n, 1024)
    hpt = tn // HEAD_DIM
    in_specs = [pl.BlockSpec((tm, d), lambda i, j: (i, 0)),
                pl.BlockSpec((d, tn), lambda i, j: (0, j))]
    args = [h, w]
    if kind != "v":
        in_specs.append(pl.BlockSpec((1, HEAD_DIM), lambda i, j: (0, 0)))
        args.append(norm_w)
    if kind == "k":
        body = functools.partial(_k_kernel, heads_per_tile=hpt, seq=seq)
        out_spec = pl.BlockSpec((hpt, tm, 2 * HEAD_DIM), lambda i, j: (j, i, 0))
        out_shape = (n // HEAD_DIM, m, 2 * HEAD_DIM)
    else:
        body = functools.partial(_qt_kernel if kind == "q" else _vt_kernel, heads_per_tile=hpt)
        rows = HEAD_DIM if kind == "q" else V_ROWS
        out_spec = pl.BlockSpec((hpt, tm // PAIR, rows, PAIR), lambda i, j: (j, i, 0, 0))
        out_shape = (n // HEAD_DIM, m // PAIR, rows, PAIR)
    return pl.pallas_call(
        body,
        grid=(m // tm, n // tn),
        in_specs=in_specs,
        out_specs=out_spec,
        out_shape=jax.ShapeDtypeStruct(out_shape, BF16),
        compiler_params=_params("parallel", "arbitrary"),
        name=f"{kind}_proj",
    )(*args)


def _conv_kernel(h_ref, wc_ref, wb_ref, wh_ref, cw_ref, o_ref, ubuf_ref, *, tiles_per_batch):
    i = pl.program_id(1)
    tm = h_ref.shape[0]

    @pl.when(i % tiles_per_batch == 0)
    def _():
        ubuf_ref[0:SUBLANES, :] = jnp.zeros((SUBLANES, ubuf_ref.shape[1]), F32)

    h = h_ref[...]
    cg = jnp.dot(h, wc_ref[...], preferred_element_type=F32)
    hc = jnp.dot(h, wh_ref[...], preferred_element_type=F32)
    ubuf_ref[SUBLANES:SUBLANES + tm, :] = cg * hc
    u0 = ubuf_ref[SUBLANES:SUBLANES + tm, :]
    u1 = ubuf_ref[SUBLANES - 1:SUBLANES - 1 + tm, :]
    u2 = ubuf_ref[SUBLANES - 2:SUBLANES - 2 + tm, :]
    y = cw_ref[0:1, :] * u2 + cw_ref[1:2, :] * u1 + cw_ref[2:3, :] * u0
    bg = jnp.dot(h, wb_ref[...], preferred_element_type=F32)
    o_ref[...] = bg * y
    ubuf_ref[0:SUBLANES, :] = ubuf_ref[tm:tm + SUBLANES, :]


def _conv(h, w_cg, w_bg, w_hc, conv_w, seq):
    m, d = h.shape
    cw = w_cg.shape[1]
    tm = _pick(seq, 1024)
    tc = _pick(cw, 256)
    wspec = pl.BlockSpec((d, tc), lambda c, i: (0, c))
    return pl.pallas_call(
        functools.partial(_conv_kernel, tiles_per_batch=seq // tm),
        grid=(cw // tc, m // tm),
        in_specs=[pl.BlockSpec((tm, d), lambda c, i: (i, 0)), wspec, wspec, wspec,
                  pl.BlockSpec((CONV_WIDTH, tc), lambda c, i: (0, c))],
        out_specs=pl.BlockSpec((tm, tc), lambda c, i: (i, c)),
        out_shape=jax.ShapeDtypeStruct((m, cw), F32),
        scratch_shapes=[pltpu.VMEM((tm + SUBLANES, tc), F32)],
        compiler_params=_params("arbitrary", "arbitrary"),
        name="conv_proj",
    )(h, w_cg, w_bg, w_hc, conv_w)


def _t5_bucket_np(dist):
    max_exact = N_BUCKETS // 2
    d = np.maximum(dist, 1).astype(np.float32)
    large = max_exact + (np.log(d / np.float32(max_exact)) / np.float32(math.log(MAX_DISTANCE / max_exact))
                         * np.float32(N_BUCKETS - max_exact)).astype(np.int32)
    large = np.minimum(large, N_BUCKETS - 1)
    return np.where(dist < max_exact, dist, large)


def _bias_plan(seq):
    buckets = _t5_bucket_np(np.arange(seq, dtype=np.int64))
    assert (np.diff(buckets) >= 0).all()
    thr = [int(np.argmax(buckets >= b)) if (buckets >= b).any() else seq for b in range(N_BUCKETS)]
    nb = seq // BLOCK
    n_tiles = nb
    for delta in range(1, nb):
        if buckets[delta * BLOCK - (BLOCK - 1)] == buckets[-1]:
            n_tiles = delta + 1
            break
    ranges = []
    for delta in range(n_tiles):
        lo = max(0, delta * BLOCK - (BLOCK - 1))
        hi = min(seq - 1, delta * BLOCK + (BLOCK - 1))
        ranges.append((int(buckets[lo]), int(buckets[hi])))
    return thr, ranges


def _work_items(nb):
    items = [(u, jj) for u in range(nb // 2) for jj in range(u + 1)]
    n = len(items)
    n_steps = n + 1
    tab = np.zeros((3, n_steps), np.int32)
    for t in range(n_steps):
        tab[0, t], tab[1, t] = items[min(t, n - 1)]
        tab[2, t] = 1 if 1 <= t <= n and items[t - 1][1] == items[t - 1][0] else 0
    return tab


def _attn_kernel(tbl_ref, item_ref, qt_ref, k_ref, vt_ref, o_ref, bias_ref, kmean_ref, qa_ref,
                 logit0_ref, logit1_ref, p0_ref, p1_ref, *, nb, thr, ranges, n_steps):
    hd = pl.program_id(0)
    b = pl.program_id(1)
    n_tiles = len(ranges)
    n_sb = nb // 2

    @pl.when(b == 0)
    def _build_bias():
        rel = (lax.broadcasted_iota(jnp.int32, (BLOCK, BLOCK), 1)
               - lax.broadcasted_iota(jnp.int32, (BLOCK, BLOCK), 0))
        for delta, (b_lo, b_hi) in enumerate(ranges):
            dist = rel + delta * BLOCK
            val = jnp.full((BLOCK, BLOCK), tbl_ref[hd, b_lo], F32)
            for bk in range(b_lo + 1, b_hi + 1):
                val = jnp.where(dist >= thr[bk], tbl_ref[hd, bk], val)
            val = val * LOG2E
            if delta == 0:
                val = jnp.where(rel >= 0, val, NEG)
            bias_ref[delta] = val

    for j in range(nb):
        kj = k_ref[j * BLOCK:(j + 1) * BLOCK, :HEAD_DIM].astype(F32)
        kmean_ref[j:j + 1, :] = jnp.sum(kj, axis=0, keepdims=True) * (1.0 / BLOCK)
    kmean = kmean_ref[...].astype(BF16)

    blk_id = lax.broadcasted_iota(jnp.int32, (nb, PAIR), 0)
    q_half = (lax.broadcasted_iota(jnp.int32, (1, PAIR), 1) >= BLOCK).astype(jnp.int32)

    def build_qa(u, carry):
        q_t = qt_ref[u]
        q_blk = 2 * u + q_half
        gate = jnp.dot(kmean, q_t, preferred_element_type=F32)
        rank = jnp.zeros((nb, PAIR), jnp.int32)
        for jp in range(nb):
            live = (q_blk > jp).astype(jnp.int32)
            tie = jnp.where(blk_id > jp, live, 0)
            g_jp = gate[jp:jp + 1, :]
            rank = rank + jnp.where(g_jp > gate, live, jnp.where(g_jp == gate, tie, 0))
        keep = jnp.where(blk_id < q_blk, rank, jnp.where(blk_id == q_blk, 0, TOPK)) < TOPK
        qa_ref[u, :HEAD_DIM] = q_t
        qa_ref[u, HEAD_DIM:HEAD_DIM + nb] = jnp.where(keep, 0.0, NEG).astype(BF16)
        return carry

    qa_ref[:, HEAD_DIM + nb:] = jnp.zeros((n_sb, LANES - nb, PAIR), BF16)
    lax.fori_loop(0, n_sb, build_qa, 0)

    def qk_logits(t, logit_ref):
        u = item_ref[0, t]
        jj = item_ref[1, t]
        r0 = pl.multiple_of(jj * PAIR, PAIR)
        raw = jnp.dot(k_ref[pl.ds(r0, PAIR), :], qa_ref[u], preferred_element_type=F32)
        d0 = 2 * (u - jj)
        tile = lambda d: bias_ref[jnp.clip(d, 0, n_tiles - 1)]
        top = raw[:BLOCK] + jnp.concatenate([tile(d0), tile(d0 + 1)], axis=1)
        bot = raw[BLOCK:] + jnp.concatenate([tile(d0 - 1), tile(d0)], axis=1)
        logit_ref[:BLOCK] = top
        logit_ref[BLOCK:] = bot
        return jnp.maximum(jnp.max(top, axis=0, keepdims=True), jnp.max(bot, axis=0, keepdims=True))

    def step(t, c, logit_cur, logit_next, p_cur, p_prev):
        cmax, alpha_prev, m, acc = c
        cmax_next = qk_logits(jnp.minimum(t + 1, n_steps - 1), logit_next)

        t_prev = jnp.maximum(t - 1, 0)
        acc = alpha_prev * acc + jnp.dot(vt_ref[item_ref[1, t_prev]], p_prev[...], preferred_element_type=F32)

        m_old = jnp.where(item_ref[1, t] == 0, NEG, m)
        m_new = jnp.maximum(m_old, cmax)
        alpha = jnp.exp2(m_old - m_new)
        p_cur[...] = jnp.exp2(logit_cur[...] - m_new).astype(BF16)
        return (cmax_next, alpha, m_new, acc), (t, acc)

    def finish(done):
        t, acc = done

        @pl.when(item_ref[2, t] == 1)
        def _():
            r0 = pl.multiple_of(item_ref[0, jnp.maximum(t - 1, 0)] * PAIR, PAIR)
            o_ref[pl.ds(r0, PAIR), :] = (acc[:HEAD_DIM] * (1.0 / acc[HEAD_DIM:HEAD_DIM + 1])).T

    logit_refs = (logit0_ref, logit1_ref)
    p_refs = (p0_ref, p1_ref)

    def steps(t0, count, c):
        done = []
        for s in range(count):
            c, d = step(t0 + s, c, logit_refs[s % 2], logit_refs[1 - s % 2], p_refs[s % 2], p_refs[1 - s % 2])
            done.append(d)
        for d in done:
            finish(d)
        return c

    p1_ref[...] = jnp.zeros_like(p1_ref)
    c = (qk_logits(0, logit0_ref), jnp.zeros((1, PAIR), F32),
         jnp.full((1, PAIR), NEG, F32), jnp.zeros((V_ROWS, PAIR), F32))
    n_main = n_steps // ATTN_UNROLL
    c = lax.fori_loop(0, n_main, lambda tt, c: steps(ATTN_UNROLL * tt, ATTN_UNROLL, c), c)
    steps(n_main * ATTN_UNROLL, n_steps % ATTN_UNROLL, c)


def _attention(qt, k, vt, rel_bias_t, batch, seq, n_heads):
    nb = seq // BLOCK
    assert nb % 2 == 0 and nb <= LANES
    thr, ranges = _bias_plan(seq)
    items = _work_items(nb)
    n_steps = items.shape[1]
    m = batch * seq
    t_spec = lambda rows: pl.BlockSpec((None, seq // PAIR, rows, PAIR), lambda h, b: (h, b, 0, 0))
    return pl.pallas_call(
        functools.partial(_attn_kernel, nb=nb, thr=thr, ranges=ranges, n_steps=n_steps),
        grid=(n_heads, batch),
        in_specs=[pl.BlockSpec(memory_space=pltpu.SMEM),
                  pl.BlockSpec(memory_space=pltpu.SMEM),
                  t_spec(HEAD_DIM),
                  pl.BlockSpec((None, seq, 2 * HEAD_DIM), lambda h, b: (h, b, 0)),
                  t_spec(V_ROWS)],
        out_specs=pl.BlockSpec((seq, HEAD_DIM), lambda h, b: (b, h)),
        out_shape=jax.ShapeDtypeStruct((m, n_heads * HEAD_DIM), F32),
        scratch_shapes=[pltpu.VMEM((len(ranges), BLOCK, BLOCK), F32),
                        pltpu.VMEM((nb, HEAD_DIM), F32),
                        pltpu.VMEM((nb // 2, 2 * LANES, PAIR), BF16),
                        pltpu.VMEM((PAIR, PAIR), F32),
                        pltpu.VMEM((PAIR, PAIR), F32),
                        pltpu.VMEM((PAIR, PAIR), BF16),
                        pltpu.VMEM((PAIR, PAIR), BF16)],
        compiler_params=_params("arbitrary", "arbitrary"),
        name="moba_attn",
    )(rel_bias_t, jnp.asarray(items), qt, k, vt)


def _outproj_kernel(attn_ref, conv_ref, wa_ref, wc_ref, w_ref, x_ref, ada_ref, o_ref, y_ref, *, k):
    j = pl.program_id(1)
    aw = attn_ref.shape[1]

    @pl.when(j == 0)
    def _():
        a = attn_ref[...]
        y_ref[:, :aw] = (a * lax.rsqrt(jnp.mean(a * a, axis=-1, keepdims=True) + EPS) * wa_ref[...]).astype(BF16)
        c = conv_ref[...]
        y_ref[:, aw:] = (c * lax.rsqrt(jnp.mean(c * c, axis=-1, keepdims=True) + EPS) * wc_ref[...]).astype(BF16)

    res = jnp.dot(y_ref[...], w_ref[...], preferred_element_type=F32)
    o_ref[...] = x_ref[...] + ada_ref[3 * k + 2:3 * k + 3, :] * res


def _outproj(attn, conv, wa, wc, w_out, x2d, ada, k, seq):
    m, aw = attn.shape
    cw = conv.shape[1]
    d = w_out.shape[1]
    tm = _pick(seq, 512)
    tn = _pick(d, 1024)
    return pl.pallas_call(
        functools.partial(_outproj_kernel, k=k),
        grid=(m // tm, d // tn),
        in_specs=[pl.BlockSpec((tm, aw), lambda i, j: (i, 0)),
                  pl.BlockSpec((tm, cw), lambda i, j: (i, 0)),
                  pl.BlockSpec((1, aw), lambda i, j: (0, 0)),
                  pl.BlockSpec((1, cw), lambda i, j: (0, 0)),
                  pl.BlockSpec((aw + cw, tn), lambda i, j: (0, j)),
                  pl.BlockSpec((tm, tn), lambda i, j: (i, j)),
                  pl.BlockSpec((None, N_ADA, tn), lambda i, j: (i * tm // seq, 0, j))],
        out_specs=pl.BlockSpec((tm, tn), lambda i, j: (i, j)),
        out_shape=jax.ShapeDtypeStruct((m, d), F32),
        scratch_shapes=[pltpu.VMEM((tm, aw + cw), BF16)],
        compiler_params=_params("parallel", "arbitrary"),
        name="out_proj",
    )(attn, conv, wa, wc, w_out, x2d, ada)


def _pad_to(a, axis, mult):
    pad = -a.shape[axis] % mult
    if not pad:
        return a
    widths = [(0, 0)] * a.ndim
    widths[axis] = (0, pad)
    return jnp.pad(a, widths)


def kernel(x, c, ada_w, ada_b, ffn1_norm, ffn1_w_gate, ffn1_w_up, ffn1_w_down, mix_norm, w_in, q_norm, k_norm,
           rel_bias, conv_w, attn_out_norm, conv_out_norm, w_out, ffn2_norm, ffn2_w_gate, ffn2_w_up, ffn2_w_down):
    batch, seq, d = x.shape
    depth = ada_w.shape[0]
    mix_w = w_out.shape[1]
    att_w = mix_w // 2
    conv_cw = mix_w - att_w
    n_heads = att_w // HEAD_DIM
    assert seq % BLOCK == 0 and att_w % HEAD_DIM == 0

    xs = x.reshape(batch * seq, d)
    c_pad = _pad_to(c, 0, SUBLANES)
    rel_bias_t = rel_bias.T
    for l in range(depth):
        ada = _ada(c_pad, ada_w[l], ada_b[l][None, :])[:batch].reshape(batch, N_ADA, d)

        h = _norm_mod(xs, ffn1_norm[l][None, :], ada, 0, seq)
        xs = _ffn(h, ffn1_w_gate[l].astype(BF16), ffn1_w_up[l].astype(BF16), ffn1_w_down[l].astype(BF16),
                  xs, ada, 0, seq)

        h = _norm_mod(xs, mix_norm[l][None, :], ada, 1, seq)
        w_in_l = w_in[l]
        qt = _head_proj(h, w_in_l[:, :att_w].astype(BF16), q_norm[l][None, :], seq, "q")
        kk = _head_proj(h, w_in_l[:, att_w:2 * att_w].astype(BF16), k_norm[l][None, :], seq, "k")
        vt = _head_proj(h, w_in_l[:, 2 * att_w:3 * att_w].astype(BF16), None, seq, "v")
        c0 = 3 * att_w
        conv = _conv(h, w_in_l[:, c0:c0 + conv_cw].astype(BF16),
                     w_in_l[:, c0 + conv_cw:c0 + 2 * conv_cw].astype(BF16),
                     w_in_l[:, c0 + 2 * conv_cw:c0 + 3 * conv_cw].astype(BF16), conv_w[l], seq)
        attn = _attention(qt, kk, vt, rel_bias_t, batch, seq, n_heads)
        xs = _outproj(attn, conv, attn_out_norm[l][None, :], conv_out_norm[l][None, :],
                      w_out[l].astype(BF16), xs, ada, 1, seq)

        h = _norm_mod(xs, ffn2_norm[l][None, :], ada, 2, seq)
        xs = _ffn(h, ffn2_w_gate[l].astype(BF16), ffn2_w_up[l].astype(BF16), ffn2_w_down[l].astype(BF16),
                  xs, ada, 2, seq)
    return xs.reshape(batch, seq, d)
```

```python
import functools
import math

import numpy as np
import jax
import jax.numpy as jnp
from jax import lax
from jax.experimental import pallas as pl
from jax.experimental.pallas import tpu as pltpu

HEAD_DIM = 128
BLOCK = 256
TOPK = 3
CONV_WIDTH = 3
N_BUCKETS = 32
MAX_DISTANCE = 2048
N_ADA = 9
EPS = 1e-6
NEG = -1e30
PAIR = 2 * BLOCK
ATTN_UNROLL = 2
V_ROWS = HEAD_DIM + 16
LOG2E = 1.4426950408889634
QK_SCALE_LOG2 = HEAD_DIM ** -0.5 * LOG2E

LANES = 128
SUBLANES = 8
VMEM_LIMIT_BYTES = 60000 * 1024

F32 = jnp.float32
BF16 = jnp.bfloat16


def _params(*sem):
    return pltpu.CompilerParams(dimension_semantics=sem, vmem_limit_bytes=VMEM_LIMIT_BYTES)


def _pick(n, pref):
    if n <= pref:
        return n
    t = pref
    while n % t:
        t //= 2
    return t


def _ada_kernel(c_ref, w_ref, b_ref, o_ref):
    c = c_ref[...]
    cond = (c * jax.nn.sigmoid(c)).astype(BF16)
    o_ref[...] = jnp.dot(cond, w_ref[...].astype(BF16), preferred_element_type=F32) + b_ref[...]


def _ada(c_pad, ada_w, ada_b):
    rows, d = c_pad.shape
    n = ada_w.shape[1]
    tn = _pick(n, 512)
    return pl.pallas_call(
        _ada_kernel,
        grid=(n // tn,),
        in_specs=[pl.BlockSpec((rows, d), lambda j: (0, 0)),
                  pl.BlockSpec((d, tn), lambda j: (0, j)),
                  pl.BlockSpec((1, tn), lambda j: (0, j))],
        out_specs=pl.BlockSpec((rows, tn), lambda j: (0, j)),
        out_shape=jax.ShapeDtypeStruct((rows, n), F32),
        compiler_params=_params("parallel"),
        name="ada_proj",
    )(c_pad, ada_w, ada_b)


def _norm_mod_kernel(x_ref, w_ref, ada_ref, o_ref, *, k):
    x = x_ref[...]
    ms = jnp.mean(x * x, axis=-1, keepdims=True)
    y = x * lax.rsqrt(ms + EPS) * w_ref[...]
    shift = ada_ref[3 * k:3 * k + 1, :]
    scale = ada_ref[3 * k + 1:3 * k + 2, :]
    o_ref[...] = (y * (1.0 + scale) + shift).astype(BF16)


def _norm_mod(x2d, w, ada, k, seq):
    m, d = x2d.shape
    tr = _pick(seq, 512)
    return pl.pallas_call(
        functools.partial(_norm_mod_kernel, k=k),
        grid=(m // tr,),
        in_specs=[pl.BlockSpec((tr, d), lambda i: (i, 0)),
                  pl.BlockSpec((1, d), lambda i: (0, 0)),
                  pl.BlockSpec((None, N_ADA, d), lambda i: (i * tr // seq, 0, 0))],
        out_specs=pl.BlockSpec((tr, d), lambda i: (i, 0)),
        out_shape=jax.ShapeDtypeStruct((m, d), BF16),
        compiler_params=_params("parallel"),
        name=f"norm_mod{k}",
    )(x2d, w, ada)


def _ffn_kernel(h_ref, wg_ref, wu_ref, wd_ref, x_ref, ada_ref, o_ref, *, k, n_xchunks, xc, nchunk, tail):
    f = pl.program_id(1)
    nf = pl.num_programs(1)
    d = o_ref.shape[1]
    tf = wg_ref.shape[1]

    @pl.when(f == 0)
    def _():
        o_ref[...] = jnp.zeros_like(o_ref)

    for c in range(n_xchunks):
        @pl.when(f == c)
        def _(c=c):
            o_ref[:, c * xc:(c + 1) * xc] += x_ref[...]

    def accumulate(width):
        h = h_ref[...]
        g = jnp.dot(h, wg_ref[:, :width], preferred_element_type=F32)
        u = jnp.dot(h, wu_ref[:, :width], preferred_element_type=F32)
        a = (g * jax.nn.sigmoid(g) * u).astype(BF16)
        half_gate = 0.5 * ada_ref[3 * k + 2:3 * k + 3, :]
        for n0 in range(0, d, nchunk):
            part = jnp.dot(a, wd_ref[:width, n0:n0 + nchunk], preferred_element_type=F32)
            o_ref[:, n0:n0 + nchunk] += half_gate[:, n0:n0 + nchunk] * part

    if tail == tf:
        accumulate(tf)
    else:
        pl.when(f < nf - 1)(lambda: accumulate(tf))
        pl.when(f == nf - 1)(lambda: accumulate(tail))


def _ffn(h, wg, wu, wd, x2d, ada, k, seq, tm_pref=512, tf_pref=512):
    m, d = h.shape
    d_ff = wg.shape[1]
    tm = _pick(seq, tm_pref)
    tf = min(tf_pref, d_ff)
    nf = pl.cdiv(d_ff, tf)
    tail = d_ff - (nf - 1) * tf
    assert tail % LANES == 0, (d_ff, tf)
    n_xchunks = 1
    while d % n_xchunks or (d // n_xchunks) % LANES or d // n_xchunks > 512:
        n_xchunks += 1
    assert n_xchunks <= nf, (n_xchunks, nf)
    xc = d // n_xchunks
    nchunk = _pick(d, 512)
    return pl.pallas_call(
        functools.partial(_ffn_kernel, k=k, n_xchunks=n_xchunks, xc=xc, nchunk=nchunk, tail=tail),
        grid=(m // tm, nf),
        in_specs=[pl.BlockSpec((tm, d), lambda i, f: (i, 0)),
                  pl.BlockSpec((d, tf), lambda i, f: (0, f)),
                  pl.BlockSpec((d, tf), lambda i, f: (0, f)),
                  pl.BlockSpec((tf, d), lambda i, f: (f, 0)),
                  pl.BlockSpec((tm, xc), lambda i, f: (i, jnp.minimum(f, n_xchunks - 1))),
                  pl.BlockSpec((None, N_ADA, d), lambda i, f: (i * tm // seq, 0, 0))],
        out_specs=pl.BlockSpec((tm, d), lambda i, f: (i, 0)),
        out_shape=jax.ShapeDtypeStruct((m, d), F32),
        compiler_params=_params("parallel", "arbitrary"),
        name=f"ffn{k}",
    )(h, wg, wu, wd, x2d, ada)


def _head_norm(blk, w):
    ms = jnp.mean(blk * blk, axis=-1, keepdims=True)
    return blk * lax.rsqrt(ms + EPS) * w


def _k_kernel(h_ref, w_ref, nw_ref, o_ref, *, heads_per_tile, seq):
    tm = h_ref.shape[0]
    res = jnp.dot(h_ref[...], w_ref[...], preferred_element_type=F32)
    pos = (pl.program_id(0) * tm) % seq + lax.broadcasted_iota(jnp.int32, (tm, LANES), 0)
    lane = lax.broadcasted_iota(jnp.int32, (tm, LANES), 1)
    onehot = jnp.where(lane * BLOCK == pos - pos % BLOCK, 1.0, 0.0).astype(BF16)
    for hh in range(heads_per_tile):
        o_ref[hh, :, :HEAD_DIM] = _head_norm(res[:, hh * HEAD_DIM:(hh + 1) * HEAD_DIM], nw_ref[...]).astype(BF16)
        o_ref[hh, :, HEAD_DIM:] = onehot


def _store_transposed(blk, o_ref, hh):
    blk_t = blk.T.astype(BF16)
    for pp in range(o_ref.shape[1]):
        o_ref[hh, pp, :HEAD_DIM] = blk_t[:, pp * PAIR:(pp + 1) * PAIR]


def _qt_kernel(h_ref, w_ref, nw_ref, o_ref, *, heads_per_tile):
    res = jnp.dot(h_ref[...], w_ref[...], preferred_element_type=F32)
    for hh in range(heads_per_tile):
        blk = _head_norm(res[:, hh * HEAD_DIM:(hh + 1) * HEAD_DIM], nw_ref[...])
        _store_transposed(blk * QK_SCALE_LOG2, o_ref, hh)


def _vt_kernel(h_ref, w_ref, o_ref, *, heads_per_tile):
    res = jnp.dot(h_ref[...], w_ref[...], preferred_element_type=F32)
    for hh in range(heads_per_tile):
        _store_transposed(res[:, hh * HEAD_DIM:(hh + 1) * HEAD_DIM], o_ref, hh)
        o_ref[hh, :, HEAD_DIM:] = jnp.ones((o_ref.shape[1], V_ROWS - HEAD_DIM, PAIR), BF16)


def _head_proj(h, w, norm_w, seq, kind):
    m, d = h.shape
    n = w.shape[1]
    tm = _pick(seq, 1024)
    tn = _pick(n, 1024)
    hpt = tn // HEAD_DIM
    in_specs = [pl.BlockSpec((tm, d), lambda i, j: (i, 0)),
                pl.BlockSpec((d, tn), lambda i, j: (0, j))]
    args = [h, w]
    if kind != "v":
        in_specs.append(pl.BlockSpec((1, HEAD_DIM), lambda i, j: (0, 0)))
        args.append(norm_w)
    if kind == "k":
        body = functools.partial(_k_kernel, heads_per_tile=hpt, seq=seq)
        out_spec = pl.BlockSpec((hpt, tm, 2 * HEAD_DIM), lambda i, j: (j, i, 0))
        out_shape = (n // HEAD_DIM, m, 2 * HEAD_DIM)
    else:
        body = functools.partial(_qt_kernel if kind == "q" else _vt_kernel, heads_per_tile=hpt)
        rows = HEAD_DIM if kind == "q" else V_ROWS
        out_spec = pl.BlockSpec((hpt, tm // PAIR, rows, PAIR), lambda i, j: (j, i, 0, 0))
        out_shape = (n // HEAD_DIM, m // PAIR, rows, PAIR)
    return pl.pallas_call(
        body,
        grid=(m // tm, n // tn),
        in_specs=in_specs,
        out_specs=out_spec,
        out_shape=jax.ShapeDtypeStruct(out_shape, BF16),
        compiler_params=_params("parallel", "arbitrary"),
        name=f"{kind}_proj",
    )(*args)


def _conv_kernel(h_ref, wc_ref, wb_ref, wh_ref, cw_ref, *refs, tiles_per_batch, n_cast):
    cast_in, o_ref, cast_out, ubuf_ref = refs[:n_cast], refs[n_cast], refs[n_cast + 1:2 * n_cast + 1], refs[-1]
    i = pl.program_id(1)
    tm = h_ref.shape[0]

    @pl.when(i % tiles_per_batch == 0)
    def _():
        ubuf_ref[0:SUBLANES, :] = jnp.zeros((SUBLANES, ubuf_ref.shape[1]), F32)

    h = h_ref[...]
    cg = jnp.dot(h, wc_ref[...], preferred_element_type=F32)
    hc = jnp.dot(h, wh_ref[...], preferred_element_type=F32)
    ubuf_ref[SUBLANES:SUBLANES + tm, :] = cg * hc
    u0 = ubuf_ref[SUBLANES:SUBLANES + tm, :]
    u1 = ubuf_ref[SUBLANES - 1:SUBLANES - 1 + tm, :]
    u2 = ubuf_ref[SUBLANES - 2:SUBLANES - 2 + tm, :]
    y = cw_ref[0:1, :] * u2 + cw_ref[1:2, :] * u1 + cw_ref[2:3, :] * u0
    bg = jnp.dot(h, wb_ref[...], preferred_element_type=F32)
    o_ref[...] = bg * y
    ubuf_ref[0:SUBLANES, :] = ubuf_ref[tm:tm + SUBLANES, :]

    for src, dst in zip(cast_in, cast_out):
        dst[...] = src[...].astype(BF16)


def _conv(h, w_cg, w_bg, w_hc, conv_w, seq, to_cast):
    m, d = h.shape
    cw = w_cg.shape[1]
    tm = _pick(seq, 1024)
    tc = _pick(cw, 256)
    n_i = m // tm
    n_steps = (cw // tc) * n_i
    wspec = pl.BlockSpec((d, tc), lambda c, i: (0, c))
    cast_specs = []
    for a in to_cast:
        rows = -(-a.shape[0] // n_steps)
        rows = -(-rows // (2 * SUBLANES)) * 2 * SUBLANES
        while a.shape[0] % rows:
            rows += 2 * SUBLANES
        last = a.shape[0] // rows - 1
        cast_specs.append(pl.BlockSpec((rows, a.shape[1]),
                                       lambda c, i, last=last: (jnp.minimum(c * n_i + i, last), 0)))
    outs = pl.pallas_call(
        functools.partial(_conv_kernel, tiles_per_batch=seq // tm, n_cast=len(to_cast)),
        grid=(cw // tc, n_i),
        in_specs=[pl.BlockSpec((tm, d), lambda c, i: (i, 0)), wspec, wspec, wspec,
                  pl.BlockSpec((CONV_WIDTH, tc), lambda c, i: (0, c))] + cast_specs,
        out_specs=[pl.BlockSpec((tm, tc), lambda c, i: (i, c))] + cast_specs,
        out_shape=[jax.ShapeDtypeStruct((m, cw), F32)] + [jax.ShapeDtypeStruct(a.shape, BF16) for a in to_cast],
        scratch_shapes=[pltpu.VMEM((tm + SUBLANES, tc), F32)],
        compiler_params=_params("arbitrary", "arbitrary"),
        name="conv_proj",
    )(h, w_cg, w_bg, w_hc, conv_w, *to_cast)
    return outs[0], outs[1:]


def _t5_bucket_np(dist):
    max_exact = N_BUCKETS // 2
    d = np.maximum(dist, 1).astype(np.float32)
    large = max_exact + (np.log(d / np.float32(max_exact)) / np.float32(math.log(MAX_DISTANCE / max_exact))
                         * np.float32(N_BUCKETS - max_exact)).astype(np.int32)
    large = np.minimum(large, N_BUCKETS - 1)
    return np.where(dist < max_exact, dist, large)


def _bias_plan(seq):
    buckets = _t5_bucket_np(np.arange(seq, dtype=np.int64))
    assert (np.diff(buckets) >= 0).all()
    thr = [int(np.argmax(buckets >= b)) if (buckets >= b).any() else seq for b in range(N_BUCKETS)]
    nb = seq // BLOCK
    n_tiles = nb
    for delta in range(1, nb):
        if buckets[delta * BLOCK - (BLOCK - 1)] == buckets[-1]:
            n_tiles = delta + 1
            break
    ranges = []
    for delta in range(n_tiles):
        lo = max(0, delta * BLOCK - (BLOCK - 1))
        hi = min(seq - 1, delta * BLOCK + (BLOCK - 1))
        ranges.append((int(buckets[lo]), int(buckets[hi])))
    return thr, ranges


def _work_items(nb):
    items = [(u, jj) for u in range(nb // 2) for jj in range(u + 1)]
    n = len(items)
    n_steps = n + 1
    tab = np.zeros((3, n_steps), np.int32)
    for t in range(n_steps):
        tab[0, t], tab[1, t] = items[min(t, n - 1)]
        tab[2, t] = 1 if 1 <= t <= n and items[t - 1][1] == items[t - 1][0] else 0
    return tab


def _attn_kernel(tbl_ref, item_ref, qt_ref, k_ref, vt_ref, o_ref, bias_ref, kmean_ref, qa_ref,
                 logit0_ref, logit1_ref, p0_ref, p1_ref, *, nb, thr, ranges, n_steps):
    hd = pl.program_id(0)
    b = pl.program_id(1)
    n_tiles = len(ranges)
    n_sb = nb // 2

    @pl.when(b == 0)
    def _build_bias():
        rel = (lax.broadcasted_iota(jnp.int32, (BLOCK, BLOCK), 1)
               - lax.broadcasted_iota(jnp.int32, (BLOCK, BLOCK), 0))
        for delta, (b_lo, b_hi) in enumerate(ranges):
            dist = rel + delta * BLOCK
            val = jnp.full((BLOCK, BLOCK), tbl_ref[hd, b_lo], F32)
            for bk in range(b_lo + 1, b_hi + 1):
                val = jnp.where(dist >= thr[bk], tbl_ref[hd, bk], val)
            val = val * LOG2E
            if delta == 0:
                val = jnp.where(rel >= 0, val, NEG)
            bias_ref[delta] = val

    for j in range(nb):
        kj = k_ref[j * BLOCK:(j + 1) * BLOCK, :HEAD_DIM].astype(F32)
        kmean_ref[j:j + 1, :] = jnp.sum(kj, axis=0, keepdims=True) * (1.0 / BLOCK)
    kmean = kmean_ref[...].astype(BF16)

    blk_id = lax.broadcasted_iota(jnp.int32, (nb, PAIR), 0)
    q_half = (lax.broadcasted_iota(jnp.int32, (1, PAIR), 1) >= BLOCK).astype(jnp.int32)

    def build_qa(u, carry):
        q_t = qt_ref[u]
        q_blk = 2 * u + q_half
        gate = jnp.dot(kmean, q_t, preferred_element_type=F32)
        rank = jnp.zeros((nb, PAIR), jnp.int32)
        for jp in range(nb):
            live = (q_blk > jp).astype(jnp.int32)
            tie = jnp.where(blk_id > jp, live, 0)
            g_jp = gate[jp:jp + 1, :]
            rank = rank + jnp.where(g_jp > gate, live, jnp.where(g_jp == gate, tie, 0))
        keep = jnp.where(blk_id < q_blk, rank, jnp.where(blk_id == q_blk, 0, TOPK)) < TOPK
        qa_ref[u, :HEAD_DIM] = q_t
        qa_ref[u, HEAD_DIM:HEAD_DIM + nb] = jnp.where(keep, 0.0, NEG).astype(BF16)
        return carry

    qa_ref[:, HEAD_DIM + nb:] = jnp.zeros((n_sb, LANES - nb, PAIR), BF16)
    lax.fori_loop(0, n_sb, build_qa, 0)

    def qk_logits(t, logit_ref):
        u = item_ref[0, t]
        jj = item_ref[1, t]
        r0 = pl.multiple_of(jj * PAIR, PAIR)
        raw = jnp.dot(k_ref[pl.ds(r0, PAIR), :], qa_ref[u], preferred_element_type=F32)
        d0 = 2 * (u - jj)
        tile = lambda d: bias_ref[jnp.clip(d, 0, n_tiles - 1)]
        top = raw[:BLOCK] + jnp.concatenate([tile(d0), tile(d0 + 1)], axis=1)
        bot = raw[BLOCK:] + jnp.concatenate([tile(d0 - 1), tile(d0)], axis=1)
        logit_ref[:BLOCK] = top
        logit_ref[BLOCK:] = bot
        return jnp.maximum(jnp.max(top, axis=0, keepdims=True), jnp.max(bot, axis=0, keepdims=True))

    def step(t, c, logit_cur, logit_next, p_cur, p_prev):
        cmax, alpha_prev, m, acc = c
        cmax_next = qk_logits(jnp.minimum(t + 1, n_steps - 1), logit_next)

        t_prev = jnp.maximum(t - 1, 0)
        acc = alpha_prev * acc + jnp.dot(vt_ref[item_ref[1, t_prev]], p_prev[...], preferred_element_type=F32)

        m_old = jnp.where(item_ref[1, t] == 0, NEG, m)
        m_new = jnp.maximum(m_old, cmax)
        alpha = jnp.exp2(m_old - m_new)
        p_cur[...] = jnp.exp2(logit_cur[...] - m_new).astype(BF16)
        return (cmax_next, alpha, m_new, acc), (t, acc)

    def finish(done):
        t, acc = done

        @pl.when(item_ref[2, t] == 1)
        def _():
            r0 = pl.multiple_of(item_ref[0, jnp.maximum(t - 1, 0)] * PAIR, PAIR)
            o_ref[pl.ds(r0, PAIR), :] = (acc[:HEAD_DIM] * (1.0 / acc[HEAD_DIM:HEAD_DIM + 1])).T

    logit_refs = (logit0_ref, logit1_ref)
    p_refs = (p0_ref, p1_ref)

    def steps(t0, count, c):
        done = []
        for s in range(count):
            c, d = step(t0 + s, c, logit_refs[s % 2], logit_refs[1 - s % 2], p_refs[s % 2], p_refs[1 - s % 2])
            done.append(d)
        for d in done:
            finish(d)
        return c

    p1_ref[...] = jnp.zeros_like(p1_ref)
    c = (qk_logits(0, logit0_ref), jnp.zeros((1, PAIR), F32),
         jnp.full((1, PAIR), NEG, F32), jnp.zeros((V_ROWS, PAIR), F32))
    n_main = n_steps // ATTN_UNROLL
    c = lax.fori_loop(0, n_main, lambda tt, c: steps(ATTN_UNROLL * tt, ATTN_UNROLL, c), c)
    steps(n_main * ATTN_UNROLL, n_steps % ATTN_UNROLL, c)


def _attention(qt, k, vt, rel_bias_t, batch, seq, n_heads):
    nb = seq // BLOCK
    assert nb % 2 == 0 and nb <= LANES
    thr, ranges = _bias_plan(seq)
    items = _work_items(nb)
    n_steps = items.shape[1]
    m = batch * seq
    t_spec = lambda rows: pl.BlockSpec((None, seq // PAIR, rows, PAIR), lambda h, b: (h, b, 0, 0))
    return pl.pallas_call(
        functools.partial(_attn_kernel, nb=nb, thr=thr, ranges=ranges, n_steps=n_steps),
        grid=(n_heads, batch),
        in_specs=[pl.BlockSpec(memory_space=pltpu.SMEM),
                  pl.BlockSpec(memory_space=pltpu.SMEM),
                  t_spec(HEAD_DIM),
                  pl.BlockSpec((None, seq, 2 * HEAD_DIM), lambda h, b: (h, b, 0)),
                  t_spec(V_ROWS)],
        out_specs=pl.BlockSpec((seq, HEAD_DIM), lambda h, b: (b, h)),
        out_shape=jax.ShapeDtypeStruct((m, n_heads * HEAD_DIM), F32),
        scratch_shapes=[pltpu.VMEM((len(ranges), BLOCK, BLOCK), F32),
                        pltpu.VMEM((nb, HEAD_DIM), F32),
                        pltpu.VMEM((nb // 2, 2 * LANES, PAIR), BF16),
                        pltpu.VMEM((PAIR, PAIR), F32),
                        pltpu.VMEM((PAIR, PAIR), F32),
                        pltpu.VMEM((PAIR, PAIR), BF16),
                        pltpu.VMEM((PAIR, PAIR), BF16)],
        compiler_params=_params("arbitrary", "arbitrary"),
        name="moba_attn",
    )(rel_bias_t, jnp.asarray(items), qt, k, vt)


def _outproj_kernel(attn_ref, conv_ref, wa_ref, wc_ref, w_ref, x_ref, ada_ref, o_ref, y_ref, *, k):
    j = pl.program_id(1)
    aw = attn_ref.shape[1]

    @pl.when(j == 0)
    def _():
        a = attn_ref[...]
        y_ref[:, :aw] = (a * lax.rsqrt(jnp.mean(a * a, axis=-1, keepdims=True) + EPS) * wa_ref[...]).astype(BF16)
        c = conv_ref[...]
        y_ref[:, aw:] = (c * lax.rsqrt(jnp.mean(c * c, axis=-1, keepdims=True) + EPS) * wc_ref[...]).astype(BF16)

    res = jnp.dot(y_ref[...], w_ref[...], preferred_element_type=F32)
    o_ref[...] = x_ref[...] + ada_ref[3 * k + 2:3 * k + 3, :] * res


def _outproj(attn, conv, wa, wc, w_out, x2d, ada, k, seq):
    m, aw = attn.shape
    cw = conv.shape[1]
    d = w_out.shape[1]
    tm = _pick(seq, 512)
    tn = _pick(d, 1024)
    return pl.pallas_call(
        functools.partial(_outproj_kernel, k=k),
        grid=(m // tm, d // tn),
        in_specs=[pl.BlockSpec((tm, aw), lambda i, j: (i, 0)),
                  pl.BlockSpec((tm, cw), lambda i, j: (i, 0)),
                  pl.BlockSpec((1, aw), lambda i, j: (0, 0)),
                  pl.BlockSpec((1, cw), lambda i, j: (0, 0)),
                  pl.BlockSpec((aw + cw, tn), lambda i, j: (0, j)),
                  pl.BlockSpec((tm, tn), lambda i, j: (i, j)),
                  pl.BlockSpec((None, N_ADA, tn), lambda i, j: (i * tm // seq, 0, j))],
        out_specs=pl.BlockSpec((tm, tn), lambda i, j: (i, j)),
        out_shape=jax.ShapeDtypeStruct((m, d), F32),
        scratch_shapes=[pltpu.VMEM((tm, aw + cw), BF16)],
        compiler_params=_params("parallel", "arbitrary"),
        name="out_proj",
    )(attn, conv, wa, wc, w_out, x2d, ada)


def _pad_to(a, axis, mult):
    pad = -a.shape[axis] % mult
    if not pad:
        return a
    widths = [(0, 0)] * a.ndim
    widths[axis] = (0, pad)
    return jnp.pad(a, widths)


def kernel(x, c, ada_w, ada_b, ffn1_norm, ffn1_w_gate, ffn1_w_up, ffn1_w_down, mix_norm, w_in, q_norm, k_norm,
           rel_bias, conv_w, attn_out_norm, conv_out_norm, w_out, ffn2_norm, ffn2_w_gate, ffn2_w_up, ffn2_w_down):
    batch, seq, d = x.shape
    depth = ada_w.shape[0]
    mix_w = w_out.shape[1]
    att_w = mix_w // 2
    conv_cw = mix_w - att_w
    n_heads = att_w // HEAD_DIM
    assert seq % BLOCK == 0 and att_w % HEAD_DIM == 0

    xs = x.reshape(batch * seq, d)
    c_pad = _pad_to(c, 0, SUBLANES)
    rel_bias_t = rel_bias.T
    for l in range(depth):
        ada = _ada(c_pad, ada_w[l], ada_b[l][None, :])[:batch].reshape(batch, N_ADA, d)

        h = _norm_mod(xs, ffn1_norm[l][None, :], ada, 0, seq)
        xs = _ffn(h, ffn1_w_gate[l].astype(BF16), ffn1_w_up[l].astype(BF16), ffn1_w_down[l].astype(BF16),
                  xs, ada, 0, seq)

        h = _norm_mod(xs, mix_norm[l][None, :], ada, 1, seq)
        w_in_l = w_in[l]
        qt = _head_proj(h, w_in_l[:, :att_w].astype(BF16), q_norm[l][None, :], seq, "q")
        kk = _head_proj(h, w_in_l[:, att_w:2 * att_w].astype(BF16), k_norm[l][None, :], seq, "k")
        vt = _head_proj(h, w_in_l[:, 2 * att_w:3 * att_w].astype(BF16), None, seq, "v")
        c0 = 3 * att_w
        conv, ffn2_w = _conv(h, w_in_l[:, c0:c0 + conv_cw].astype(BF16),
                             w_in_l[:, c0 + conv_cw:c0 + 2 * conv_cw].astype(BF16),
                             w_in_l[:, c0 + 2 * conv_cw:c0 + 3 * conv_cw].astype(BF16), conv_w[l], seq,
                             [ffn2_w_gate[l], ffn2_w_up[l], ffn2_w_down[l]])
        attn = _attention(qt, kk, vt, rel_bias_t, batch, seq, n_heads)
        xs = _outproj(attn, conv, attn_out_norm[l][None, :], conv_out_norm[l][None, :],
                      w_out[l].astype(BF16), xs, ada, 1, seq)

        h = _norm_mod(xs, ffn2_norm[l][None, :], ada, 2, seq)
        xs = _ffn(h, *ffn2_w, xs, ada, 2, seq)
    return xs.reshape(batch, seq, d)
```

```python
import functools
import math

import numpy as np
import jax
import jax.numpy as jnp
from jax import lax
from jax.experimental import pallas as pl
from jax.experimental.pallas import tpu as pltpu

HEAD_DIM = 128
BLOCK = 256
TOPK = 3
CONV_WIDTH = 3
N_BUCKETS = 32
MAX_DISTANCE = 2048
N_ADA = 9
EPS = 1e-6
NEG = -1e30
PAIR = 2 * BLOCK
ATTN_UNROLL = 2
V_ROWS = HEAD_DIM + 16
LOG2E = 1.4426950408889634
QK_SCALE_LOG2 = HEAD_DIM ** -0.5 * LOG2E

LANES = 128
SUBLANES = 8
VMEM_LIMIT_BYTES = 60000 * 1024

F32 = jnp.float32
BF16 = jnp.bfloat16


def _params(*sem):
    return pltpu.CompilerParams(dimension_semantics=sem, vmem_limit_bytes=VMEM_LIMIT_BYTES)


def _pick(n, pref):
    if n <= pref:
        return n
    t = pref
    while n % t:
        t //= 2
    return t


def _cast_specs(to_cast, n_steps, step_of):
    specs = []
    for a in to_cast:
        rows = -(-a.shape[0] // n_steps)
        rows = -(-rows // (2 * SUBLANES)) * 2 * SUBLANES
        while a.shape[0] % rows:
            rows += 2 * SUBLANES
        last = a.shape[0] // rows - 1
        specs.append(pl.BlockSpec((rows, a.shape[1]), lambda *g, last=last: (jnp.minimum(step_of(*g), last), 0)))
    return specs


def _cast_streams(cast_in, cast_out):
    for src, dst in zip(cast_in, cast_out):
        dst[...] = src[...].astype(BF16)


def _ada_kernel(c_ref, w_ref, b_ref, o_ref):
    c = c_ref[...]
    cond = (c * jax.nn.sigmoid(c)).astype(BF16)
    o_ref[...] = jnp.dot(cond, w_ref[...].astype(BF16), preferred_element_type=F32) + b_ref[...]


def _ada(c_pad, ada_w, ada_b):
    rows, d = c_pad.shape
    n = ada_w.shape[1]
    tn = _pick(n, 512)
    return pl.pallas_call(
        _ada_kernel,
        grid=(n // tn,),
        in_specs=[pl.BlockSpec((rows, d), lambda j: (0, 0)),
                  pl.BlockSpec((d, tn), lambda j: (0, j)),
                  pl.BlockSpec((1, tn), lambda j: (0, j))],
        out_specs=pl.BlockSpec((rows, tn), lambda j: (0, j)),
        out_shape=jax.ShapeDtypeStruct((rows, n), F32),
        compiler_params=_params("parallel"),
        name="ada_proj",
    )(c_pad, ada_w, ada_b)


def _norm_mod_kernel(x_ref, w_ref, ada_ref, o_ref, *, k):
    x = x_ref[...]
    ms = jnp.mean(x * x, axis=-1, keepdims=True)
    y = x * lax.rsqrt(ms + EPS) * w_ref[...]
    shift = ada_ref[3 * k:3 * k + 1, :]
    scale = ada_ref[3 * k + 1:3 * k + 2, :]
    o_ref[...] = (y * (1.0 + scale) + shift).astype(BF16)


def _norm_mod(x2d, w, ada, k, seq):
    m, d = x2d.shape
    tr = _pick(seq, 512)
    return pl.pallas_call(
        functools.partial(_norm_mod_kernel, k=k),
        grid=(m // tr,),
        in_specs=[pl.BlockSpec((tr, d), lambda i: (i, 0)),
                  pl.BlockSpec((1, d), lambda i: (0, 0)),
                  pl.BlockSpec((None, N_ADA, d), lambda i: (i * tr // seq, 0, 0))],
        out_specs=pl.BlockSpec((tr, d), lambda i: (i, 0)),
        out_shape=jax.ShapeDtypeStruct((m, d), BF16),
        compiler_params=_params("parallel"),
        name=f"norm_mod{k}",
    )(x2d, w, ada)


def _ffn_kernel(h_ref, wg_ref, wu_ref, wd_ref, x_ref, ada_ref, *refs, k, n_xchunks, xc, nchunk, tail, n_cast):
    cast_in, o_ref, cast_out = refs[:n_cast], refs[n_cast], refs[n_cast + 1:]
    _cast_streams(cast_in, cast_out)
    f = pl.program_id(1)
    nf = pl.num_programs(1)
    d = o_ref.shape[1]
    tf = wg_ref.shape[1]

    @pl.when(f == 0)
    def _():
        o_ref[...] = jnp.zeros_like(o_ref)

    for c in range(n_xchunks):
        @pl.when(f == c)
        def _(c=c):
            o_ref[:, c * xc:(c + 1) * xc] += x_ref[...]

    def accumulate(width):
        h = h_ref[...]
        g = jnp.dot(h, wg_ref[:, :width], preferred_element_type=F32)
        u = jnp.dot(h, wu_ref[:, :width], preferred_element_type=F32)
        a = (g * jax.nn.sigmoid(g) * u).astype(BF16)
        half_gate = 0.5 * ada_ref[3 * k + 2:3 * k + 3, :]
        for n0 in range(0, d, nchunk):
            part = jnp.dot(a, wd_ref[:width, n0:n0 + nchunk], preferred_element_type=F32)
            o_ref[:, n0:n0 + nchunk] += half_gate[:, n0:n0 + nchunk] * part

    if tail == tf:
        accumulate(tf)
    else:
        pl.when(f < nf - 1)(lambda: accumulate(tf))
        pl.when(f == nf - 1)(lambda: accumulate(tail))


def _ffn(h, wg, wu, wd, x2d, ada, k, seq, to_cast=(), tm_pref=512, tf_pref=512):
    m, d = h.shape
    d_ff = wg.shape[1]
    tm = _pick(seq, tm_pref)
    tf = min(tf_pref, d_ff)
    nf = pl.cdiv(d_ff, tf)
    tail = d_ff - (nf - 1) * tf
    assert tail % LANES == 0, (d_ff, tf)
    n_xchunks = 1
    while d % n_xchunks or (d // n_xchunks) % LANES or d // n_xchunks > 512:
        n_xchunks += 1
    assert n_xchunks <= nf, (n_xchunks, nf)
    xc = d // n_xchunks
    nchunk = _pick(d, 512)
    cast_specs = _cast_specs(to_cast, (m // tm) * nf, lambda i, f: i * nf + f)
    outs = pl.pallas_call(
        functools.partial(_ffn_kernel, k=k, n_xchunks=n_xchunks, xc=xc, nchunk=nchunk, tail=tail,
                          n_cast=len(to_cast)),
        grid=(m // tm, nf),
        in_specs=[pl.BlockSpec((tm, d), lambda i, f: (i, 0)),
                  pl.BlockSpec((d, tf), lambda i, f: (0, f)),
                  pl.BlockSpec((d, tf), lambda i, f: (0, f)),
                  pl.BlockSpec((tf, d), lambda i, f: (f, 0)),
                  pl.BlockSpec((tm, xc), lambda i, f: (i, jnp.minimum(f, n_xchunks - 1))),
                  pl.BlockSpec((None, N_ADA, d), lambda i, f: (i * tm // seq, 0, 0))] + cast_specs,
        out_specs=[pl.BlockSpec((tm, d), lambda i, f: (i, 0))] + cast_specs,
        out_shape=[jax.ShapeDtypeStruct((m, d), F32)] + [jax.ShapeDtypeStruct(a.shape, BF16) for a in to_cast],
        compiler_params=_params("arbitrary", "arbitrary"),
        name=f"ffn{k}",
    )(h, wg, wu, wd, x2d, ada, *to_cast)
    return outs[0], outs[1:]


def _head_norm(blk, w):
    ms = jnp.mean(blk * blk, axis=-1, keepdims=True)
    return blk * lax.rsqrt(ms + EPS) * w


def _k_kernel(h_ref, w_ref, nw_ref, o_ref, *, heads_per_tile, seq):
    tm = h_ref.shape[0]
    res = jnp.dot(h_ref[...], w_ref[...], preferred_element_type=F32)
    pos = (pl.program_id(0) * tm) % seq + lax.broadcasted_iota(jnp.int32, (tm, LANES), 0)
    lane = lax.broadcasted_iota(jnp.int32, (tm, LANES), 1)
    onehot = jnp.where(lane * BLOCK == pos - pos % BLOCK, 1.0, 0.0).astype(BF16)
    for hh in range(heads_per_tile):
        o_ref[hh, :, :HEAD_DIM] = _head_norm(res[:, hh * HEAD_DIM:(hh + 1) * HEAD_DIM], nw_ref[...]).astype(BF16)
        o_ref[hh, :, HEAD_DIM:] = onehot


def _store_transposed(blk, o_ref, hh):
    blk_t = blk.T.astype(BF16)
    for pp in range(o_ref.shape[1]):
        o_ref[hh, pp, :HEAD_DIM] = blk_t[:, pp * PAIR:(pp + 1) * PAIR]


def _qt_kernel(h_ref, w_ref, nw_ref, o_ref, *, heads_per_tile):
    res = jnp.dot(h_ref[...], w_ref[...], preferred_element_type=F32)
    for hh in range(heads_per_tile):
        blk = _head_norm(res[:, hh * HEAD_DIM:(hh + 1) * HEAD_DIM], nw_ref[...])
        _store_transposed(blk * QK_SCALE_LOG2, o_ref, hh)


def _vt_kernel(h_ref, w_ref, o_ref, *, heads_per_tile):
    res = jnp.dot(h_ref[...], w_ref[...], preferred_element_type=F32)
    for hh in range(heads_per_tile):
        _store_transposed(res[:, hh * HEAD_DIM:(hh + 1) * HEAD_DIM], o_ref, hh)
        o_ref[hh, :, HEAD_DIM:] = jnp.ones((o_ref.shape[1], V_ROWS - HEAD_DIM, PAIR), BF16)


def _head_proj(h, w, col0, n, norm_w, seq, kind):
    m, d = h.shape
    tm = _pick(seq, 1024)
    tn = _pick(n, 1024)
    assert col0 % tn == 0
    hpt = tn // HEAD_DIM
    in_specs = [pl.BlockSpec((tm, d), lambda i, j: (i, 0)),
                pl.BlockSpec((d, tn), lambda i, j: (0, col0 // tn + j))]
    args = [h, w]
    if kind != "v":
        in_specs.append(pl.BlockSpec((1, HEAD_DIM), lambda i, j: (0, 0)))
        args.append(norm_w)
    if kind == "k":
        body = functools.partial(_k_kernel, heads_per_tile=hpt, seq=seq)
        out_spec = pl.BlockSpec((hpt, tm, 2 * HEAD_DIM), lambda i, j: (j, i, 0))
        out_shape = (n // HEAD_DIM, m, 2 * HEAD_DIM)
    else:
        body = functools.partial(_qt_kernel if kind == "q" else _vt_kernel, heads_per_tile=hpt)
        rows = HEAD_DIM if kind == "q" else V_ROWS
        out_spec = pl.BlockSpec((hpt, tm // PAIR, rows, PAIR), lambda i, j: (j, i, 0, 0))
        out_shape = (n // HEAD_DIM, m // PAIR, rows, PAIR)
    return pl.pallas_call(
        body,
        grid=(m // tm, n // tn),
        in_specs=in_specs,
        out_specs=out_spec,
        out_shape=jax.ShapeDtypeStruct(out_shape, BF16),
        compiler_params=_params("parallel", "arbitrary"),
        name=f"{kind}_proj",
    )(*args)


def _conv_kernel(h_ref, wc_ref, wb_ref, wh_ref, cw_ref, *refs, tiles_per_batch, n_cast):
    cast_in, o_ref, cast_out, ubuf_ref = refs[:n_cast], refs[n_cast], refs[n_cast + 1:2 * n_cast + 1], refs[-1]
    _cast_streams(cast_in, cast_out)
    i = pl.program_id(1)
    tm = h_ref.shape[0]

    @pl.when(i % tiles_per_batch == 0)
    def _():
        ubuf_ref[0:SUBLANES, :] = jnp.zeros((SUBLANES, ubuf_ref.shape[1]), F32)

    h = h_ref[...]
    cg = jnp.dot(h, wc_ref[...], preferred_element_type=F32)
    hc = jnp.dot(h, wh_ref[...], preferred_element_type=F32)
    ubuf_ref[SUBLANES:SUBLANES + tm, :] = cg * hc
    u0 = ubuf_ref[SUBLANES:SUBLANES + tm, :]
    u1 = ubuf_ref[SUBLANES - 1:SUBLANES - 1 + tm, :]
    u2 = ubuf_ref[SUBLANES - 2:SUBLANES - 2 + tm, :]
    y = cw_ref[0:1, :] * u2 + cw_ref[1:2, :] * u1 + cw_ref[2:3, :] * u0
    bg = jnp.dot(h, wb_ref[...], preferred_element_type=F32)
    o_ref[...] = bg * y
    ubuf_ref[0:SUBLANES, :] = ubuf_ref[tm:tm + SUBLANES, :]


def _conv(h, w, col0, cw, conv_w, seq, to_cast):
    m, d = h.shape
    tm = _pick(seq, 1024)
    tc = _pick(cw, 256)
    assert col0 % tc == 0
    n_i = m // tm
    wspec = lambda group: pl.BlockSpec((d, tc), lambda c, i: (0, (col0 + group * cw) // tc + c))
    cast_specs = _cast_specs(to_cast, (cw // tc) * n_i, lambda c, i: c * n_i + i)
    outs = pl.pallas_call(
        functools.partial(_conv_kernel, tiles_per_batch=seq // tm, n_cast=len(to_cast)),
        grid=(cw // tc, n_i),
        in_specs=[pl.BlockSpec((tm, d), lambda c, i: (i, 0)), wspec(0), wspec(1), wspec(2),
                  pl.BlockSpec((CONV_WIDTH, tc), lambda c, i: (0, c))] + cast_specs,
        out_specs=[pl.BlockSpec((tm, tc), lambda c, i: (i, c))] + cast_specs,
        out_shape=[jax.ShapeDtypeStruct((m, cw), F32)] + [jax.ShapeDtypeStruct(a.shape, BF16) for a in to_cast],
        scratch_shapes=[pltpu.VMEM((tm + SUBLANES, tc), F32)],
        compiler_params=_params("arbitrary", "arbitrary"),
        name="conv_proj",
    )(h, w, w, w, conv_w, *to_cast)
    return outs[0], outs[1:]


def _t5_bucket_np(dist):
    max_exact = N_BUCKETS // 2
    d = np.maximum(dist, 1).astype(np.float32)
    large = max_exact + (np.log(d / np.float32(max_exact)) / np.float32(math.log(MAX_DISTANCE / max_exact))
                         * np.float32(N_BUCKETS - max_exact)).astype(np.int32)
    large = np.minimum(large, N_BUCKETS - 1)
    return np.where(dist < max_exact, dist, large)


def _bias_plan(seq):
    buckets = _t5_bucket_np(np.arange(seq, dtype=np.int64))
    assert (np.diff(buckets) >= 0).all()
    thr = [int(np.argmax(buckets >= b)) if (buckets >= b).any() else seq for b in range(N_BUCKETS)]
    nb = seq // BLOCK
    n_tiles = nb
    for delta in range(1, nb):
        if buckets[delta * BLOCK - (BLOCK - 1)] == buckets[-1]:
            n_tiles = delta + 1
            break
    ranges = []
    for delta in range(n_tiles):
        lo = max(0, delta * BLOCK - (BLOCK - 1))
        hi = min(seq - 1, delta * BLOCK + (BLOCK - 1))
        ranges.append((int(buckets[lo]), int(buckets[hi])))
    return thr, ranges


def _work_items(nb):
    items = [(u, jj) for u in range(nb // 2) for jj in range(u + 1)]
    n = len(items)
    n_steps = n + 1
    tab = np.zeros((3, n_steps), np.int32)
    for t in range(n_steps):
        tab[0, t], tab[1, t] = items[min(t, n - 1)]
        tab[2, t] = 1 if 1 <= t <= n and items[t - 1][1] == items[t - 1][0] else 0
    return tab


def _attn_kernel(tbl_ref, item_ref, qt_ref, k_ref, vt_ref, o_ref, bias_ref, kmean_ref, qa_ref,
                 logit0_ref, logit1_ref, p0_ref, p1_ref, *, nb, thr, ranges, n_steps):
    hd = pl.program_id(0)
    b = pl.program_id(1)
    n_tiles = len(ranges)
    n_sb = nb // 2

    @pl.when(b == 0)
    def _build_bias():
        rel = (lax.broadcasted_iota(jnp.int32, (BLOCK, BLOCK), 1)
               - lax.broadcasted_iota(jnp.int32, (BLOCK, BLOCK), 0))
        for delta, (b_lo, b_hi) in enumerate(ranges):
            dist = rel + delta * BLOCK
            val = jnp.full((BLOCK, BLOCK), tbl_ref[hd, b_lo], F32)
            for bk in range(b_lo + 1, b_hi + 1):
                val = jnp.where(dist >= thr[bk], tbl_ref[hd, bk], val)
            val = val * LOG2E
            if delta == 0:
                val = jnp.where(rel >= 0, val, NEG)
            bias_ref[delta] = val

    for j in range(nb):
        kj = k_ref[j * BLOCK:(j + 1) * BLOCK, :HEAD_DIM].astype(F32)
        kmean_ref[j:j + 1, :] = jnp.sum(kj, axis=0, keepdims=True) * (1.0 / BLOCK)
    kmean = kmean_ref[...].astype(BF16)

    blk_id = lax.broadcasted_iota(jnp.int32, (nb, PAIR), 0)
    q_half = (lax.broadcasted_iota(jnp.int32, (1, PAIR), 1) >= BLOCK).astype(jnp.int32)

    def build_qa(u, carry):
        q_t = qt_ref[u]
        q_blk = 2 * u + q_half
        gate = jnp.dot(kmean, q_t, preferred_element_type=F32)
        rank = jnp.zeros((nb, PAIR), jnp.int32)
        for jp in range(nb):
            live = (q_blk > jp).astype(jnp.int32)
            tie = jnp.where(blk_id > jp, live, 0)
            g_jp = gate[jp:jp + 1, :]
            rank = rank + jnp.where(g_jp > gate, live, jnp.where(g_jp == gate, tie, 0))
        keep = jnp.where(blk_id < q_blk, rank, jnp.where(blk_id == q_blk, 0, TOPK)) < TOPK
        qa_ref[u, :HEAD_DIM] = q_t
        qa_ref[u, HEAD_DIM:HEAD_DIM + nb] = jnp.where(keep, 0.0, NEG).astype(BF16)
        return carry

    qa_ref[:, HEAD_DIM + nb:] = jnp.zeros((n_sb, LANES - nb, PAIR), BF16)
    lax.fori_loop(0, n_sb, build_qa, 0)

    def qk_logits(t, logit_ref):
        u = item_ref[0, t]
        jj = item_ref[1, t]
        r0 = pl.multiple_of(jj * PAIR, PAIR)
        raw = jnp.dot(k_ref[pl.ds(r0, PAIR), :], qa_ref[u], preferred_element_type=F32)
        d0 = 2 * (u - jj)
        tile = lambda d: bias_ref[jnp.clip(d, 0, n_tiles - 1)]
        top = raw[:BLOCK] + jnp.concatenate([tile(d0), tile(d0 + 1)], axis=1)
        bot = raw[BLOCK:] + jnp.concatenate([tile(d0 - 1), tile(d0)], axis=1)
        logit_ref[:BLOCK] = top
        logit_ref[BLOCK:] = bot
        return jnp.maximum(jnp.max(top, axis=0, keepdims=True), jnp.max(bot, axis=0, keepdims=True))

    def step(t, c, logit_cur, logit_next, p_cur, p_prev):
        cmax, alpha_prev, m, acc = c
        cmax_next = qk_logits(jnp.minimum(t + 1, n_steps - 1), logit_next)

        t_prev = jnp.maximum(t - 1, 0)
        acc = alpha_prev * acc + jnp.dot(vt_ref[item_ref[1, t_prev]], p_prev[...], preferred_element_type=F32)

        m_old = jnp.where(item_ref[1, t] == 0, NEG, m)
        m_new = jnp.maximum(m_old, cmax)
        alpha = jnp.exp2(m_old - m_new)
        p_cur[...] = jnp.exp2(logit_cur[...] - m_new).astype(BF16)
        return (cmax_next, alpha, m_new, acc), (t, acc)

    def finish(done):
        t, acc = done

        @pl.when(item_ref[2, t] == 1)
        def _():
            r0 = pl.multiple_of(item_ref[0, jnp.maximum(t - 1, 0)] * PAIR, PAIR)
            o_ref[pl.ds(r0, PAIR), :] = (acc[:HEAD_DIM] * (1.0 / acc[HEAD_DIM:HEAD_DIM + 1])).T

    logit_refs = (logit0_ref, logit1_ref)
    p_refs = (p0_ref, p1_ref)

    def steps(t0, count, c):
        done = []
        for s in range(count):
            c, d = step(t0 + s, c, logit_refs[s % 2], logit_refs[1 - s % 2], p_refs[s % 2], p_refs[1 - s % 2])
            done.append(d)
        for d in done:
            finish(d)
        return c

    p1_ref[...] = jnp.zeros_like(p1_ref)
    c = (qk_logits(0, logit0_ref), jnp.zeros((1, PAIR), F32),
         jnp.full((1, PAIR), NEG, F32), jnp.zeros((V_ROWS, PAIR), F32))
    n_main = n_steps // ATTN_UNROLL
    c = lax.fori_loop(0, n_main, lambda tt, c: steps(ATTN_UNROLL * tt, ATTN_UNROLL, c), c)
    steps(n_main * ATTN_UNROLL, n_steps % ATTN_UNROLL, c)


def _attention(qt, k, vt, rel_bias_t, batch, seq, n_heads):
    nb = seq // BLOCK
    assert nb % 2 == 0 and nb <= LANES
    thr, ranges = _bias_plan(seq)
    items = _work_items(nb)
    n_steps = items.shape[1]
    m = batch * seq
    t_spec = lambda rows: pl.BlockSpec((None, seq // PAIR, rows, PAIR), lambda h, b: (h, b, 0, 0))
    return pl.pallas_call(
        functools.partial(_attn_kernel, nb=nb, thr=thr, ranges=ranges, n_steps=n_steps),
        grid=(n_heads, batch),
        in_specs=[pl.BlockSpec(memory_space=pltpu.SMEM),
                  pl.BlockSpec(memory_space=pltpu.SMEM),
                  t_spec(HEAD_DIM),
                  pl.BlockSpec((None, seq, 2 * HEAD_DIM), lambda h, b: (h, b, 0)),
                  t_spec(V_ROWS)],
        out_specs=pl.BlockSpec((seq, HEAD_DIM), lambda h, b: (b, h)),
        out_shape=jax.ShapeDtypeStruct((m, n_heads * HEAD_DIM), F32),
        scratch_shapes=[pltpu.VMEM((len(ranges), BLOCK, BLOCK), F32),
                        pltpu.VMEM((nb, HEAD_DIM), F32),
                        pltpu.VMEM((nb // 2, 2 * LANES, PAIR), BF16),
                        pltpu.VMEM((PAIR, PAIR), F32),
                        pltpu.VMEM((PAIR, PAIR), F32),
                        pltpu.VMEM((PAIR, PAIR), BF16),
                        pltpu.VMEM((PAIR, PAIR), BF16)],
        compiler_params=_params("arbitrary", "arbitrary"),
        name="moba_attn",
    )(rel_bias_t, jnp.asarray(items), qt, k, vt)


def _outproj_kernel(attn_ref, conv_ref, wa_ref, wc_ref, w_ref, x_ref, ada_ref, o_ref, y_ref, *, k):
    j = pl.program_id(1)
    aw = attn_ref.shape[1]

    @pl.when(j == 0)
    def _():
        a = attn_ref[...]
        y_ref[:, :aw] = (a * lax.rsqrt(jnp.mean(a * a, axis=-1, keepdims=True) + EPS) * wa_ref[...]).astype(BF16)
        c = conv_ref[...]
        y_ref[:, aw:] = (c * lax.rsqrt(jnp.mean(c * c, axis=-1, keepdims=True) + EPS) * wc_ref[...]).astype(BF16)

    res = jnp.dot(y_ref[...], w_ref[...], preferred_element_type=F32)
    o_ref[...] = x_ref[...] + ada_ref[3 * k + 2:3 * k + 3, :] * res


def _outproj(attn, conv, wa, wc, w_out, x2d, ada, k, seq):
    m, aw = attn.shape
    cw = conv.shape[1]
    d = w_out.shape[1]
    tm = _pick(seq, 512)
    tn = _pick(d, 1024)
    return pl.pallas_call(
        functools.partial(_outproj_kernel, k=k),
        grid=(m // tm, d // tn),
        in_specs=[pl.BlockSpec((tm, aw), lambda i, j: (i, 0)),
                  pl.BlockSpec((tm, cw), lambda i, j: (i, 0)),
                  pl.BlockSpec((1, aw), lambda i, j: (0, 0)),
                  pl.BlockSpec((1, cw), lambda i, j: (0, 0)),
                  pl.BlockSpec((aw + cw, tn), lambda i, j: (0, j)),
                  pl.BlockSpec((tm, tn), lambda i, j: (i, j)),
                  pl.BlockSpec((None, N_ADA, tn), lambda i, j: (i * tm // seq, 0, j))],
        out_specs=pl.BlockSpec((tm, tn), lambda i, j: (i, j)),
        out_shape=jax.ShapeDtypeStruct((m, d), F32),
        scratch_shapes=[pltpu.VMEM((tm, aw + cw), BF16)],
        compiler_params=_params("parallel", "arbitrary"),
        name="out_proj",
    )(attn, conv, wa, wc, w_out, x2d, ada)


def _pad_to(a, axis, mult):
    pad = -a.shape[axis] % mult
    if not pad:
        return a
    widths = [(0, 0)] * a.ndim
    widths[axis] = (0, pad)
    return jnp.pad(a, widths)


def kernel(x, c, ada_w, ada_b, ffn1_norm, ffn1_w_gate, ffn1_w_up, ffn1_w_down, mix_norm, w_in, q_norm, k_norm,
           rel_bias, conv_w, attn_out_norm, conv_out_norm, w_out, ffn2_norm, ffn2_w_gate, ffn2_w_up, ffn2_w_down):
    batch, seq, d = x.shape
    depth = ada_w.shape[0]
    mix_w = w_out.shape[1]
    att_w = mix_w // 2
    conv_cw = mix_w - att_w
    n_heads = att_w // HEAD_DIM
    assert seq % BLOCK == 0 and att_w % HEAD_DIM == 0

    xs = x.reshape(batch * seq, d)
    c_pad = _pad_to(c, 0, SUBLANES)
    rel_bias_t = rel_bias.T
    for l in range(depth):
        ada = _ada(c_pad, ada_w[l], ada_b[l][None, :])[:batch].reshape(batch, N_ADA, d)

        h = _norm_mod(xs, ffn1_norm[l][None, :], ada, 0, seq)
        xs, (w_in_l,) = _ffn(h, ffn1_w_gate[l].astype(BF16), ffn1_w_up[l].astype(BF16), ffn1_w_down[l].astype(BF16),
                             xs, ada, 0, seq, to_cast=[w_in[l]])

        h = _norm_mod(xs, mix_norm[l][None, :], ada, 1, seq)
        qt = _head_proj(h, w_in_l, 0, att_w, q_norm[l][None, :], seq, "q")
        kk = _head_proj(h, w_in_l, att_w, att_w, k_norm[l][None, :], seq, "k")
        vt = _head_proj(h, w_in_l, 2 * att_w, att_w, None, seq, "v")
        conv, (wg2, wu2, wd2, w_out_l) = _conv(h, w_in_l, 3 * att_w, conv_cw, conv_w[l], seq,
                                               [ffn2_w_gate[l], ffn2_w_up[l], ffn2_w_down[l], w_out[l]])
        attn = _attention(qt, kk, vt, rel_bias_t, batch, seq, n_heads)
        xs = _outproj(attn, conv, attn_out_norm[l][None, :], conv_out_norm[l][None, :], w_out_l, xs, ada, 1, seq)

        h = _norm_mod(xs, ffn2_norm[l][None, :], ada, 2, seq)
        xs, _ = _ffn(h, wg2, wu2, wd2, xs, ada, 2, seq)
    return xs.reshape(batch, seq, d)
```

```python
import functools
import math

import numpy as np
import jax
import jax.numpy as jnp
from jax import lax
from jax.experimental import pallas as pl
from jax.experimental.pallas import tpu as pltpu

HEAD_DIM = 128
BLOCK = 256
TOPK = 3
CONV_WIDTH = 3
N_BUCKETS = 32
MAX_DISTANCE = 2048
N_ADA = 9
EPS = 1e-6
NEG = -1e30
PAIR = 2 * BLOCK
ATTN_UNROLL = 2
V_ROWS = HEAD_DIM + 16
FFN_TILE = 512
LOG2E = 1.4426950408889634
QK_SCALE_LOG2 = HEAD_DIM ** -0.5 * LOG2E

LANES = 128
SUBLANES = 8
VMEM_LIMIT_BYTES = 60000 * 1024

F32 = jnp.float32
BF16 = jnp.bfloat16


def _params(*sem):
    return pltpu.CompilerParams(dimension_semantics=sem, vmem_limit_bytes=VMEM_LIMIT_BYTES)


def _pick(n, pref):
    if n <= pref:
        return n
    t = pref
    while n % t:
        t //= 2
    return t


def _cast_plan(to_cast, n_steps, step_of):
    in_specs, out_specs, out_shapes = [], [], []
    for a, tile in to_cast:
        rows = -(-a.shape[0] // n_steps)
        rows = -(-rows // (2 * SUBLANES)) * 2 * SUBLANES
        while a.shape[0] % rows:
            rows += 2 * SUBLANES
        block = lambda *g, last=a.shape[0] // rows - 1: jnp.minimum(step_of(*g), last)
        in_specs.append(pl.BlockSpec((rows, a.shape[1]), lambda *g, block=block: (block(*g), 0)))
        if tile is None:
            out_specs.append(in_specs[-1])
            out_shapes.append(jax.ShapeDtypeStruct(a.shape, BF16))
        else:
            n_tiles = -(-a.shape[1] // tile)
            out_specs.append(pl.BlockSpec((n_tiles, rows, tile), lambda *g, block=block: (0, block(*g), 0)))
            out_shapes.append(jax.ShapeDtypeStruct((n_tiles, a.shape[0], tile), BF16))
    return in_specs, out_specs, out_shapes


def _cast_streams(cast_in, cast_out):
    for src, dst in zip(cast_in, cast_out):
        if len(dst.shape) == 2:
            dst[...] = src[...].astype(BF16)
            continue
        n_tiles, rows, tile = dst.shape
        for t in range(n_tiles):
            width = min(tile, src.shape[1] - t * tile)
            dst[t, :, :width] = src[:, t * tile:t * tile + width].astype(BF16)
            if width < tile:
                dst[t, :, width:] = jnp.zeros((rows, tile - width), BF16)


def _ada_kernel(c_ref, w_ref, b_ref, o_ref):
    c = c_ref[...]
    cond = (c * jax.nn.sigmoid(c)).astype(BF16)
    o_ref[...] = jnp.dot(cond, w_ref[...].astype(BF16), preferred_element_type=F32) + b_ref[...]


def _ada(c_pad, ada_w, ada_b):
    rows, d = c_pad.shape
    n = ada_w.shape[1]
    tn = _pick(n, 512)
    return pl.pallas_call(
        _ada_kernel,
        grid=(n // tn,),
        in_specs=[pl.BlockSpec((rows, d), lambda j: (0, 0)),
                  pl.BlockSpec((d, tn), lambda j: (0, j)),
                  pl.BlockSpec((1, tn), lambda j: (0, j))],
        out_specs=pl.BlockSpec((rows, tn), lambda j: (0, j)),
        out_shape=jax.ShapeDtypeStruct((rows, n), F32),
        compiler_params=_params("parallel"),
        name="ada_proj",
    )(c_pad, ada_w, ada_b)


def _norm_mod_kernel(x_ref, w_ref, ada_ref, o_ref, *, k):
    x = x_ref[...]
    ms = jnp.mean(x * x, axis=-1, keepdims=True)
    y = x * lax.rsqrt(ms + EPS) * w_ref[...]
    shift = ada_ref[3 * k:3 * k + 1, :]
    scale = ada_ref[3 * k + 1:3 * k + 2, :]
    o_ref[...] = (y * (1.0 + scale) + shift).astype(BF16)


def _norm_mod(x2d, w, ada, k, seq):
    m, d = x2d.shape
    tr = _pick(seq, 512)
    return pl.pallas_call(
        functools.partial(_norm_mod_kernel, k=k),
        grid=(m // tr,),
        in_specs=[pl.BlockSpec((tr, d), lambda i: (i, 0)),
                  pl.BlockSpec((1, d), lambda i: (0, 0)),
                  pl.BlockSpec((None, N_ADA, d), lambda i: (i * tr // seq, 0, 0))],
        out_specs=pl.BlockSpec((tr, d), lambda i: (i, 0)),
        out_shape=jax.ShapeDtypeStruct((m, d), BF16),
        compiler_params=_params("parallel"),
        name=f"norm_mod{k}",
    )(x2d, w, ada)


def _ffn_kernel(h_ref, wg_ref, wu_ref, wd_ref, x_ref, ada_ref, *refs, k, n_xchunks, xc, nchunk, tail, n_cast):
    cast_in, o_ref, cast_out = refs[:n_cast], refs[n_cast], refs[n_cast + 1:]
    _cast_streams(cast_in, cast_out)
    f = pl.program_id(1)
    nf = pl.num_programs(1)
    d = o_ref.shape[1]
    tf = wg_ref.shape[1]

    @pl.when(f == 0)
    def _():
        o_ref[...] = jnp.zeros_like(o_ref)

    for c in range(n_xchunks):
        @pl.when(f == c)
        def _(c=c):
            o_ref[:, c * xc:(c + 1) * xc] += x_ref[...]

    def accumulate(width):
        h = h_ref[...]
        g = jnp.dot(h, wg_ref[:, :width], preferred_element_type=F32)
        u = jnp.dot(h, wu_ref[:, :width], preferred_element_type=F32)
        a = (g * jax.nn.sigmoid(g) * u).astype(BF16)
        half_gate = 0.5 * ada_ref[3 * k + 2:3 * k + 3, :]
        for n0 in range(0, d, nchunk):
            part = jnp.dot(a, wd_ref[:width, n0:n0 + nchunk], preferred_element_type=F32)
            o_ref[:, n0:n0 + nchunk] += half_gate[:, n0:n0 + nchunk] * part

    if tail == tf:
        accumulate(tf)
    else:
        pl.when(f < nf - 1)(lambda: accumulate(tf))
        pl.when(f == nf - 1)(lambda: accumulate(tail))


def _ffn(h, wg, wu, wd, x2d, ada, k, seq, to_cast=(), tm_pref=512):
    m, d = h.shape
    d_ff = wd.shape[0]
    tm = _pick(seq, tm_pref)
    tf = min(FFN_TILE, d_ff)
    nf = pl.cdiv(d_ff, tf)
    tail = d_ff - (nf - 1) * tf
    assert tail % LANES == 0, (d_ff, tf)
    n_xchunks = 1
    while d % n_xchunks or (d // n_xchunks) % LANES or d // n_xchunks > 512:
        n_xchunks += 1
    assert n_xchunks <= nf, (n_xchunks, nf)
    xc = d // n_xchunks
    nchunk = _pick(d, 512)
    cast_in, cast_out, cast_shapes = _cast_plan(to_cast, (m // tm) * nf, lambda i, f: i * nf + f)
    if wg.ndim == 3:
        assert wg.shape == wu.shape == (nf, d, tf), (wg.shape, wu.shape)
        w_spec = pl.BlockSpec((None, d, tf), lambda i, f: (f, 0, 0))
    else:
        w_spec = pl.BlockSpec((d, tf), lambda i, f: (0, f))
    outs = pl.pallas_call(
        functools.partial(_ffn_kernel, k=k, n_xchunks=n_xchunks, xc=xc, nchunk=nchunk, tail=tail,
                          n_cast=len(to_cast)),
        grid=(m // tm, nf),
        in_specs=[pl.BlockSpec((tm, d), lambda i, f: (i, 0)),
                  w_spec,
                  w_spec,
                  pl.BlockSpec((tf, d), lambda i, f: (f, 0)),
                  pl.BlockSpec((tm, xc), lambda i, f: (i, jnp.minimum(f, n_xchunks - 1))),
                  pl.BlockSpec((None, N_ADA, d), lambda i, f: (i * tm // seq, 0, 0))] + cast_in,
        out_specs=[pl.BlockSpec((tm, d), lambda i, f: (i, 0))] + cast_out,
        out_shape=[jax.ShapeDtypeStruct((m, d), F32)] + cast_shapes,
        compiler_params=_params("arbitrary", "arbitrary"),
        name=f"ffn{k}",
    )(h, wg, wu, wd, x2d, ada, *[a for a, _ in to_cast])
    return outs[0], outs[1:]


def _head_norm(blk, w):
    ms = jnp.mean(blk * blk, axis=-1, keepdims=True)
    return blk * lax.rsqrt(ms + EPS) * w


def _k_kernel(h_ref, w_ref, nw_ref, o_ref, *, heads_per_tile, seq):
    tm = h_ref.shape[0]
    res = jnp.dot(h_ref[...], w_ref[...], preferred_element_type=F32)
    pos = (pl.program_id(0) * tm) % seq + lax.broadcasted_iota(jnp.int32, (tm, LANES), 0)
    lane = lax.broadcasted_iota(jnp.int32, (tm, LANES), 1)
    onehot = jnp.where(lane * BLOCK == pos - pos % BLOCK, 1.0, 0.0).astype(BF16)
    for hh in range(heads_per_tile):
        o_ref[hh, :, :HEAD_DIM] = _head_norm(res[:, hh * HEAD_DIM:(hh + 1) * HEAD_DIM], nw_ref[...]).astype(BF16)
        o_ref[hh, :, HEAD_DIM:] = onehot


def _store_transposed(blk, o_ref, hh):
    blk_t = blk.T.astype(BF16)
    for pp in range(o_ref.shape[1]):
        o_ref[hh, pp, :HEAD_DIM] = blk_t[:, pp * PAIR:(pp + 1) * PAIR]


def _qt_kernel(h_ref, w_ref, nw_ref, o_ref, *, heads_per_tile):
    res = jnp.dot(h_ref[...], w_ref[...], preferred_element_type=F32)
    for hh in range(heads_per_tile):
        blk = _head_norm(res[:, hh * HEAD_DIM:(hh + 1) * HEAD_DIM], nw_ref[...])
        _store_transposed(blk * QK_SCALE_LOG2, o_ref, hh)


def _vt_kernel(h_ref, w_ref, o_ref, *, heads_per_tile):
    res = jnp.dot(h_ref[...], w_ref[...], preferred_element_type=F32)
    for hh in range(heads_per_tile):
        _store_transposed(res[:, hh * HEAD_DIM:(hh + 1) * HEAD_DIM], o_ref, hh)
        o_ref[hh, :, HEAD_DIM:] = jnp.ones((o_ref.shape[1], V_ROWS - HEAD_DIM, PAIR), BF16)


def _head_proj(h, w, col0, n, norm_w, seq, kind):
    m, d = h.shape
    tm = _pick(seq, 1024)
    tn = _pick(n, 1024)
    assert col0 % tn == 0
    hpt = tn // HEAD_DIM
    in_specs = [pl.BlockSpec((tm, d), lambda i, j: (i, 0)),
                pl.BlockSpec((d, tn), lambda i, j: (0, col0 // tn + j))]
    args = [h, w]
    if kind != "v":
        in_specs.append(pl.BlockSpec((1, HEAD_DIM), lambda i, j: (0, 0)))
        args.append(norm_w)
    if kind == "k":
        body = functools.partial(_k_kernel, heads_per_tile=hpt, seq=seq)
        out_spec = pl.BlockSpec((hpt, tm, 2 * HEAD_DIM), lambda i, j: (j, i, 0))
        out_shape = (n // HEAD_DIM, m, 2 * HEAD_DIM)
    else:
        body = functools.partial(_qt_kernel if kind == "q" else _vt_kernel, heads_per_tile=hpt)
        rows = HEAD_DIM if kind == "q" else V_ROWS
        out_spec = pl.BlockSpec((hpt, tm // PAIR, rows, PAIR), lambda i, j: (j, i, 0, 0))
        out_shape = (n // HEAD_DIM, m // PAIR, rows, PAIR)
    return pl.pallas_call(
        body,
        grid=(m // tm, n // tn),
        in_specs=in_specs,
        out_specs=out_spec,
        out_shape=jax.ShapeDtypeStruct(out_shape, BF16),
        compiler_params=_params("parallel", "arbitrary"),
        name=f"{kind}_proj",
    )(*args)


def _conv_kernel(h_ref, wc_ref, wb_ref, wh_ref, cw_ref, *refs, tiles_per_batch, n_cast):
    cast_in, o_ref, cast_out, ubuf_ref = refs[:n_cast], refs[n_cast], refs[n_cast + 1:2 * n_cast + 1], refs[-1]
    _cast_streams(cast_in, cast_out)
    i = pl.program_id(1)
    tm = h_ref.shape[0]

    @pl.when(i % tiles_per_batch == 0)
    def _():
        ubuf_ref[0:SUBLANES, :] = jnp.zeros((SUBLANES, ubuf_ref.shape[1]), F32)

    h = h_ref[...]
    cg = jnp.dot(h, wc_ref[...], preferred_element_type=F32)
    hc = jnp.dot(h, wh_ref[...], preferred_element_type=F32)
    ubuf_ref[SUBLANES:SUBLANES + tm, :] = cg * hc
    u0 = ubuf_ref[SUBLANES:SUBLANES + tm, :]
    u1 = ubuf_ref[SUBLANES - 1:SUBLANES - 1 + tm, :]
    u2 = ubuf_ref[SUBLANES - 2:SUBLANES - 2 + tm, :]
    y = cw_ref[0:1, :] * u2 + cw_ref[1:2, :] * u1 + cw_ref[2:3, :] * u0
    bg = jnp.dot(h, wb_ref[...], preferred_element_type=F32)
    o_ref[...] = bg * y
    ubuf_ref[0:SUBLANES, :] = ubuf_ref[tm:tm + SUBLANES, :]


def _conv(h, w, col0, cw, conv_w, seq, to_cast):
    m, d = h.shape
    tm = _pick(seq, 1024)
    tc = _pick(cw, 256)
    assert col0 % tc == 0
    n_i = m // tm
    wspec = lambda group: pl.BlockSpec((d, tc), lambda c, i: (0, (col0 + group * cw) // tc + c))
    cast_in, cast_out, cast_shapes = _cast_plan(to_cast, (cw // tc) * n_i, lambda c, i: c * n_i + i)
    outs = pl.pallas_call(
        functools.partial(_conv_kernel, tiles_per_batch=seq // tm, n_cast=len(to_cast)),
        grid=(cw // tc, n_i),
        in_specs=[pl.BlockSpec((tm, d), lambda c, i: (i, 0)), wspec(0), wspec(1), wspec(2),
                  pl.BlockSpec((CONV_WIDTH, tc), lambda c, i: (0, c))] + cast_in,
        out_specs=[pl.BlockSpec((tm, tc), lambda c, i: (i, c))] + cast_out,
        out_shape=[jax.ShapeDtypeStruct((m, cw), F32)] + cast_shapes,
        scratch_shapes=[pltpu.VMEM((tm + SUBLANES, tc), F32)],
        compiler_params=_params("arbitrary", "arbitrary"),
        name="conv_proj",
    )(h, w, w, w, conv_w, *[a for a, _ in to_cast])
    return outs[0], outs[1:]


def _t5_bucket_np(dist):
    max_exact = N_BUCKETS // 2
    d = np.maximum(dist, 1).astype(np.float32)
    large = max_exact + (np.log(d / np.float32(max_exact)) / np.float32(math.log(MAX_DISTANCE / max_exact))
                         * np.float32(N_BUCKETS - max_exact)).astype(np.int32)
    large = np.minimum(large, N_BUCKETS - 1)
    return np.where(dist < max_exact, dist, large)


def _bias_plan(seq):
    buckets = _t5_bucket_np(np.arange(seq, dtype=np.int64))
    assert (np.diff(buckets) >= 0).all()
    thr = [int(np.argmax(buckets >= b)) if (buckets >= b).any() else seq for b in range(N_BUCKETS)]
    nb = seq // BLOCK
    n_tiles = nb
    for delta in range(1, nb):
        if buckets[delta * BLOCK - (BLOCK - 1)] == buckets[-1]:
            n_tiles = delta + 1
            break
    ranges = []
    for delta in range(n_tiles):
        lo = max(0, delta * BLOCK - (BLOCK - 1))
        hi = min(seq - 1, delta * BLOCK + (BLOCK - 1))
        ranges.append((int(buckets[lo]), int(buckets[hi])))
    return thr, ranges


def _work_items(nb):
    items = [(u, jj) for u in range(nb // 2) for jj in range(u + 1)]
    n = len(items)
    n_steps = n + 1
    tab = np.zeros((3, n_steps), np.int32)
    for t in range(n_steps):
        tab[0, t], tab[1, t] = items[min(t, n - 1)]
        tab[2, t] = 1 if 1 <= t <= n and items[t - 1][1] == items[t - 1][0] else 0
    return tab


def _attn_kernel(tbl_ref, item_ref, qt_ref, k_ref, vt_ref, o_ref, bias_ref, kmean_ref, qa_ref,
                 logit0_ref, logit1_ref, p0_ref, p1_ref, *, nb, thr, ranges, n_steps):
    hd = pl.program_id(0)
    b = pl.program_id(1)
    n_tiles = len(ranges)
    n_sb = nb // 2

    @pl.when(b == 0)
    def _build_bias():
        rel = (lax.broadcasted_iota(jnp.int32, (BLOCK, BLOCK), 1)
               - lax.broadcasted_iota(jnp.int32, (BLOCK, BLOCK), 0))
        for delta, (b_lo, b_hi) in enumerate(ranges):
            dist = rel + delta * BLOCK
            val = jnp.full((BLOCK, BLOCK), tbl_ref[hd, b_lo], F32)
            for bk in range(b_lo + 1, b_hi + 1):
                val = jnp.where(dist >= thr[bk], tbl_ref[hd, bk], val)
            val = val * LOG2E
            if delta == 0:
                val = jnp.where(rel >= 0, val, NEG)
            bias_ref[delta] = val

    for j in range(nb):
        kj = k_ref[j * BLOCK:(j + 1) * BLOCK, :HEAD_DIM].astype(F32)
        kmean_ref[j:j + 1, :] = jnp.sum(kj, axis=0, keepdims=True) * (1.0 / BLOCK)
    kmean = kmean_ref[...].astype(BF16)

    blk_id = lax.broadcasted_iota(jnp.int32, (nb, PAIR), 0)
    q_half = (lax.broadcasted_iota(jnp.int32, (1, PAIR), 1) >= BLOCK).astype(jnp.int32)

    def build_qa(u, carry):
        q_t = qt_ref[u]
        q_blk = 2 * u + q_half
        gate = jnp.dot(kmean, q_t, preferred_element_type=F32)
        rank = jnp.zeros((nb, PAIR), jnp.int32)
        for jp in range(nb):
            live = (q_blk > jp).astype(jnp.int32)
            tie = jnp.where(blk_id > jp, live, 0)
            g_jp = gate[jp:jp + 1, :]
            rank = rank + jnp.where(g_jp > gate, live, jnp.where(g_jp == gate, tie, 0))
        keep = jnp.where(blk_id < q_blk, rank, jnp.where(blk_id == q_blk, 0, TOPK)) < TOPK
        qa_ref[u, :HEAD_DIM] = q_t
        qa_ref[u, HEAD_DIM:HEAD_DIM + nb] = jnp.where(keep, 0.0, NEG).astype(BF16)
        return carry

    qa_ref[:, HEAD_DIM + nb:] = jnp.zeros((n_sb, LANES - nb, PAIR), BF16)
    lax.fori_loop(0, n_sb, build_qa, 0)

    def qk_logits(t, logit_ref):
        u = item_ref[0, t]
        jj = item_ref[1, t]
        r0 = pl.multiple_of(jj * PAIR, PAIR)
        raw = jnp.dot(k_ref[pl.ds(r0, PAIR), :], qa_ref[u], preferred_element_type=F32)
        d0 = 2 * (u - jj)
        tile = lambda d: bias_ref[jnp.clip(d, 0, n_tiles - 1)]
        top = raw[:BLOCK] + jnp.concatenate([tile(d0), tile(d0 + 1)], axis=1)
        bot = raw[BLOCK:] + jnp.concatenate([tile(d0 - 1), tile(d0)], axis=1)
        logit_ref[:BLOCK] = top
        logit_ref[BLOCK:] = bot
        return jnp.maximum(jnp.max(top, axis=0, keepdims=True), jnp.max(bot, axis=0, keepdims=True))

    def step(t, c, logit_cur, logit_next, p_cur, p_prev):
        cmax, alpha_prev, m, acc = c
        cmax_next = qk_logits(jnp.minimum(t + 1, n_steps - 1), logit_next)

        t_prev = jnp.maximum(t - 1, 0)
        acc = alpha_prev * acc + jnp.dot(vt_ref[item_ref[1, t_prev]], p_prev[...], preferred_element_type=F32)

        m_old = jnp.where(item_ref[1, t] == 0, NEG, m)
        m_new = jnp.maximum(m_old, cmax)
        alpha = jnp.exp2(m_old - m_new)
        p_cur[...] = jnp.exp2(logit_cur[...] - m_new).astype(BF16)
        return (cmax_next, alpha, m_new, acc), (t, acc)

    def finish(done):
        t, acc = done

        @pl.when(item_ref[2, t] == 1)
        def _():
            r0 = pl.multiple_of(item_ref[0, jnp.maximum(t - 1, 0)] * PAIR, PAIR)
            o_ref[pl.ds(r0, PAIR), :] = (acc[:HEAD_DIM] * (1.0 / acc[HEAD_DIM:HEAD_DIM + 1])).T

    logit_refs = (logit0_ref, logit1_ref)
    p_refs = (p0_ref, p1_ref)

    def steps(t0, count, c):
        done = []
        for s in range(count):
            c, d = step(t0 + s, c, logit_refs[s % 2], logit_refs[1 - s % 2], p_refs[s % 2], p_refs[1 - s % 2])
            done.append(d)
        for d in done:
            finish(d)
        return c

    p1_ref[...] = jnp.zeros_like(p1_ref)
    c = (qk_logits(0, logit0_ref), jnp.zeros((1, PAIR), F32),
         jnp.full((1, PAIR), NEG, F32), jnp.zeros((V_ROWS, PAIR), F32))
    n_main = n_steps // ATTN_UNROLL
    c = lax.fori_loop(0, n_main, lambda tt, c: steps(ATTN_UNROLL * tt, ATTN_UNROLL, c), c)
    steps(n_main * ATTN_UNROLL, n_steps % ATTN_UNROLL, c)


def _attention(qt, k, vt, rel_bias_t, batch, seq, n_heads):
    nb = seq // BLOCK
    assert nb % 2 == 0 and nb <= LANES
    thr, ranges = _bias_plan(seq)
    items = _work_items(nb)
    n_steps = items.shape[1]
    m = batch * seq
    t_spec = lambda rows: pl.BlockSpec((None, seq // PAIR, rows, PAIR), lambda h, b: (h, b, 0, 0))
    return pl.pallas_call(
        functools.partial(_attn_kernel, nb=nb, thr=thr, ranges=ranges, n_steps=n_steps),
        grid=(n_heads, batch),
        in_specs=[pl.BlockSpec(memory_space=pltpu.SMEM),
                  pl.BlockSpec(memory_space=pltpu.SMEM),
                  t_spec(HEAD_DIM),
                  pl.BlockSpec((None, seq, 2 * HEAD_DIM), lambda h, b: (h, b, 0)),
                  t_spec(V_ROWS)],
        out_specs=pl.BlockSpec((seq, HEAD_DIM), lambda h, b: (b, h)),
        out_shape=jax.ShapeDtypeStruct((m, n_heads * HEAD_DIM), F32),
        scratch_shapes=[pltpu.VMEM((len(ranges), BLOCK, BLOCK), F32),
                        pltpu.VMEM((nb, HEAD_DIM), F32),
                        pltpu.VMEM((nb // 2, 2 * LANES, PAIR), BF16),
                        pltpu.VMEM((PAIR, PAIR), F32),
                        pltpu.VMEM((PAIR, PAIR), F32),
                        pltpu.VMEM((PAIR, PAIR), BF16),
                        pltpu.VMEM((PAIR, PAIR), BF16)],
        compiler_params=_params("arbitrary", "arbitrary"),
        name="moba_attn",
    )(rel_bias_t, jnp.asarray(items), qt, k, vt)


def _outproj_kernel(attn_ref, conv_ref, wa_ref, wc_ref, w_ref, x_ref, ada_ref, o_ref, y_ref, *, k):
    j = pl.program_id(1)
    aw = attn_ref.shape[1]

    @pl.when(j == 0)
    def _():
        a = attn_ref[...]
        y_ref[:, :aw] = (a * lax.rsqrt(jnp.mean(a * a, axis=-1, keepdims=True) + EPS) * wa_ref[...]).astype(BF16)
        c = conv_ref[...]
        y_ref[:, aw:] = (c * lax.rsqrt(jnp.mean(c * c, axis=-1, keepdims=True) + EPS) * wc_ref[...]).astype(BF16)

    res = jnp.dot(y_ref[...], w_ref[...], preferred_element_type=F32)
    o_ref[...] = x_ref[...] + ada_ref[3 * k + 2:3 * k + 3, :] * res


def _outproj(attn, conv, wa, wc, w_out, x2d, ada, k, seq):
    m, aw = attn.shape
    cw = conv.shape[1]
    d = w_out.shape[1]
    tm = _pick(seq, 512)
    tn = _pick(d, 1024)
    return pl.pallas_call(
        functools.partial(_outproj_kernel, k=k),
        grid=(m // tm, d // tn),
        in_specs=[pl.BlockSpec((tm, aw), lambda i, j: (i, 0)),
                  pl.BlockSpec((tm, cw), lambda i, j: (i, 0)),
                  pl.BlockSpec((1, aw), lambda i, j: (0, 0)),
                  pl.BlockSpec((1, cw), lambda i, j: (0, 0)),
                  pl.BlockSpec((aw + cw, tn), lambda i, j: (0, j)),
                  pl.BlockSpec((tm, tn), lambda i, j: (i, j)),
                  pl.BlockSpec((None, N_ADA, tn), lambda i, j: (i * tm // seq, 0, j))],
        out_specs=pl.BlockSpec((tm, tn), lambda i, j: (i, j)),
        out_shape=jax.ShapeDtypeStruct((m, d), F32),
        scratch_shapes=[pltpu.VMEM((tm, aw + cw), BF16)],
        compiler_params=_params("parallel", "arbitrary"),
        name="out_proj",
    )(attn, conv, wa, wc, w_out, x2d, ada)


def _pad_to(a, axis, mult):
    pad = -a.shape[axis] % mult
    if not pad:
        return a
    widths = [(0, 0)] * a.ndim
    widths[axis] = (0, pad)
    return jnp.pad(a, widths)


def kernel(x, c, ada_w, ada_b, ffn1_norm, ffn1_w_gate, ffn1_w_up, ffn1_w_down, mix_norm, w_in, q_norm, k_norm,
           rel_bias, conv_w, attn_out_norm, conv_out_norm, w_out, ffn2_norm, ffn2_w_gate, ffn2_w_up, ffn2_w_down):
    batch, seq, d = x.shape
    depth = ada_w.shape[0]
    mix_w = w_out.shape[1]
    att_w = mix_w // 2
    conv_cw = mix_w - att_w
    n_heads = att_w // HEAD_DIM
    assert seq % BLOCK == 0 and att_w % HEAD_DIM == 0

    xs = x.reshape(batch * seq, d)
    c_pad = _pad_to(c, 0, SUBLANES)
    rel_bias_t = rel_bias.T
    for l in range(depth):
        ada = _ada(c_pad, ada_w[l], ada_b[l][None, :])[:batch].reshape(batch, N_ADA, d)

        h = _norm_mod(xs, ffn1_norm[l][None, :], ada, 0, seq)
        xs, (w_in_l,) = _ffn(h, ffn1_w_gate[l].astype(BF16), ffn1_w_up[l].astype(BF16), ffn1_w_down[l].astype(BF16),
                             xs, ada, 0, seq, to_cast=[(w_in[l], None)])

        h = _norm_mod(xs, mix_norm[l][None, :], ada, 1, seq)
        qt = _head_proj(h, w_in_l, 0, att_w, q_norm[l][None, :], seq, "q")
        kk = _head_proj(h, w_in_l, att_w, att_w, k_norm[l][None, :], seq, "k")
        vt = _head_proj(h, w_in_l, 2 * att_w, att_w, None, seq, "v")
        conv, (wg2, wu2, wd2, w_out_l) = _conv(h, w_in_l, 3 * att_w, conv_cw, conv_w[l], seq,
                                               [(ffn2_w_gate[l], FFN_TILE), (ffn2_w_up[l], FFN_TILE),
                                                (ffn2_w_down[l], None), (w_out[l], None)])
        attn = _attention(qt, kk, vt, rel_bias_t, batch, seq, n_heads)
        xs = _outproj(attn, conv, attn_out_norm[l][None, :], conv_out_norm[l][None, :], w_out_l, xs, ada, 1, seq)

        h = _norm_mod(xs, ffn2_norm[l][None, :], ada, 2, seq)
        xs, _ = _ffn(h, wg2, wu2, wd2, xs, ada, 2, seq)
    return xs.reshape(batch, seq, d)
```

```python
import functools
import math

import numpy as np
import jax
import jax.numpy as jnp
from jax import lax
from jax.experimental import pallas as pl
from jax.experimental.pallas import tpu as pltpu

HEAD_DIM = 128
BLOCK = 256
TOPK = 3
CONV_WIDTH = 3
N_BUCKETS = 32
MAX_DISTANCE = 2048
N_ADA = 9
EPS = 1e-6
NEG = -1e30
PAIR = 2 * BLOCK
ATTN_UNROLL = 2
V_ROWS = HEAD_DIM + 16
FFN_TILE = 512
LOG2E = 1.4426950408889634
QK_SCALE_LOG2 = HEAD_DIM ** -0.5 * LOG2E

LANES = 128
SUBLANES = 8
VMEM_LIMIT_BYTES = 60000 * 1024

F32 = jnp.float32
BF16 = jnp.bfloat16


def _params(*sem):
    return pltpu.CompilerParams(dimension_semantics=sem, vmem_limit_bytes=VMEM_LIMIT_BYTES)


def _pick(n, pref):
    if n <= pref:
        return n
    t = pref
    while n % t:
        t //= 2
    return t


def _cast_plan(to_cast, n_steps, step_of):
    in_specs, out_specs, out_shapes = [], [], []
    for a, tile in to_cast:
        rows = -(-a.shape[0] // n_steps)
        rows = -(-rows // (2 * SUBLANES)) * 2 * SUBLANES
        while a.shape[0] % rows:
            rows += 2 * SUBLANES
        block = lambda *g, last=a.shape[0] // rows - 1: jnp.minimum(step_of(*g), last)
        in_specs.append(pl.BlockSpec((rows, a.shape[1]), lambda *g, block=block: (block(*g), 0)))
        if tile is None:
            out_specs.append(in_specs[-1])
            out_shapes.append(jax.ShapeDtypeStruct(a.shape, BF16))
        else:
            n_tiles = -(-a.shape[1] // tile)
            out_specs.append(pl.BlockSpec((n_tiles, rows, tile), lambda *g, block=block: (0, block(*g), 0)))
            out_shapes.append(jax.ShapeDtypeStruct((n_tiles, a.shape[0], tile), BF16))
    return in_specs, out_specs, out_shapes


def _cast_streams(cast_in, cast_out):
    for src, dst in zip(cast_in, cast_out):
        if len(dst.shape) == 2:
            dst[...] = src[...].astype(BF16)
            continue
        n_tiles, rows, tile = dst.shape
        for t in range(n_tiles):
            width = min(tile, src.shape[1] - t * tile)
            dst[t, :, :width] = src[:, t * tile:t * tile + width].astype(BF16)
            if width < tile:
                dst[t, :, width:] = jnp.zeros((rows, tile - width), BF16)


def _ada_kernel(c_ref, w_ref, b_ref, o_ref):
    c = c_ref[...]
    cond = (c * jax.nn.sigmoid(c)).astype(BF16)
    o_ref[...] = jnp.dot(cond, w_ref[...].astype(BF16), preferred_element_type=F32) + b_ref[...]


def _ada(c_pad, ada_w, ada_b):
    rows, d = c_pad.shape
    n = ada_w.shape[1]
    tn = _pick(n, 512)
    return pl.pallas_call(
        _ada_kernel,
        grid=(n // tn,),
        in_specs=[pl.BlockSpec((rows, d), lambda j: (0, 0)),
                  pl.BlockSpec((d, tn), lambda j: (0, j)),
                  pl.BlockSpec((1, tn), lambda j: (0, j))],
        out_specs=pl.BlockSpec((rows, tn), lambda j: (0, j)),
        out_shape=jax.ShapeDtypeStruct((rows, n), F32),
        compiler_params=_params("parallel"),
        name="ada_proj",
    )(c_pad, ada_w, ada_b)


def _norm_mod_kernel(x_ref, w_ref, ada_ref, o_ref, *, k):
    x = x_ref[...]
    ms = jnp.mean(x * x, axis=-1, keepdims=True)
    y = x * lax.rsqrt(ms + EPS) * w_ref[...]
    shift = ada_ref[3 * k:3 * k + 1, :]
    scale = ada_ref[3 * k + 1:3 * k + 2, :]
    o_ref[...] = (y * (1.0 + scale) + shift).astype(BF16)


def _norm_mod(x2d, w, ada, k, seq):
    m, d = x2d.shape
    tr = _pick(seq, 512)
    return pl.pallas_call(
        functools.partial(_norm_mod_kernel, k=k),
        grid=(m // tr,),
        in_specs=[pl.BlockSpec((tr, d), lambda i: (i, 0)),
                  pl.BlockSpec((1, d), lambda i: (0, 0)),
                  pl.BlockSpec((None, N_ADA, d), lambda i: (i * tr // seq, 0, 0))],
        out_specs=pl.BlockSpec((tr, d), lambda i: (i, 0)),
        out_shape=jax.ShapeDtypeStruct((m, d), BF16),
        compiler_params=_params("parallel"),
        name=f"norm_mod{k}",
    )(x2d, w, ada)


def _ffn_kernel(h_ref, wg_ref, wu_ref, wd_ref, x_ref, ada_ref, *refs, k, n_xchunks, xc, nchunk, tail, n_cast):
    cast_in, o_ref, cast_out = refs[:n_cast], refs[n_cast], refs[n_cast + 1:]
    _cast_streams(cast_in, cast_out)
    f = pl.program_id(1)
    nf = pl.num_programs(1)
    d = o_ref.shape[1]
    tf = wg_ref.shape[1]

    @pl.when(f == 0)
    def _():
        o_ref[...] = jnp.zeros_like(o_ref)

    for c in range(n_xchunks):
        @pl.when(f == c)
        def _(c=c):
            o_ref[:, c * xc:(c + 1) * xc] += x_ref[...]

    def accumulate(width):
        h = h_ref[...]
        g = jnp.dot(h, wg_ref[:, :width], preferred_element_type=F32)
        u = jnp.dot(h, wu_ref[:, :width], preferred_element_type=F32)
        a = (g * jax.nn.sigmoid(g) * u).astype(BF16)
        half_gate = 0.5 * ada_ref[3 * k + 2:3 * k + 3, :]
        for n0 in range(0, d, nchunk):
            part = jnp.dot(a, wd_ref[:width, n0:n0 + nchunk], preferred_element_type=F32)
            o_ref[:, n0:n0 + nchunk] += half_gate[:, n0:n0 + nchunk] * part

    if tail == tf:
        accumulate(tf)
    else:
        pl.when(f < nf - 1)(lambda: accumulate(tf))
        pl.when(f == nf - 1)(lambda: accumulate(tail))


def _ffn(h, wg, wu, wd, x2d, ada, k, seq, to_cast=(), tm_pref=512):
    m, d = h.shape
    d_ff = wd.shape[0]
    tm = _pick(seq, tm_pref)
    tf = min(FFN_TILE, d_ff)
    nf = pl.cdiv(d_ff, tf)
    tail = d_ff - (nf - 1) * tf
    assert tail % LANES == 0, (d_ff, tf)
    n_xchunks = 1
    while d % n_xchunks or (d // n_xchunks) % LANES or d // n_xchunks > 512:
        n_xchunks += 1
    assert n_xchunks <= nf, (n_xchunks, nf)
    xc = d // n_xchunks
    nchunk = _pick(d, 512)
    cast_in, cast_out, cast_shapes = _cast_plan(to_cast, (m // tm) * nf, lambda i, f: i * nf + f)
    if wg.ndim == 3:
        assert wg.shape == wu.shape == (nf, d, tf), (wg.shape, wu.shape)
        w_spec = pl.BlockSpec((None, d, tf), lambda i, f: (f, 0, 0))
    else:
        w_spec = pl.BlockSpec((d, tf), lambda i, f: (0, f))
    outs = pl.pallas_call(
        functools.partial(_ffn_kernel, k=k, n_xchunks=n_xchunks, xc=xc, nchunk=nchunk, tail=tail,
                          n_cast=len(to_cast)),
        grid=(m // tm, nf),
        in_specs=[pl.BlockSpec((tm, d), lambda i, f: (i, 0)),
                  w_spec,
                  w_spec,
                  pl.BlockSpec((tf, d), lambda i, f: (f, 0)),
                  pl.BlockSpec((tm, xc), lambda i, f: (i, jnp.minimum(f, n_xchunks - 1))),
                  pl.BlockSpec((None, N_ADA, d), lambda i, f: (i * tm // seq, 0, 0))] + cast_in,
        out_specs=[pl.BlockSpec((tm, d), lambda i, f: (i, 0))] + cast_out,
        out_shape=[jax.ShapeDtypeStruct((m, d), F32)] + cast_shapes,
        compiler_params=_params("arbitrary", "arbitrary"),
        name=f"ffn{k}",
    )(h, wg, wu, wd, x2d, ada, *[a for a, _ in to_cast])
    return outs[0], outs[1:]


def _head_norm(blk, w):
    ms = jnp.mean(blk * blk, axis=-1, keepdims=True)
    return blk * lax.rsqrt(ms + EPS) * w


def _k_kernel(h_ref, w_ref, nw_ref, o_ref, *, heads_per_tile, seq):
    tm = h_ref.shape[0]
    res = jnp.dot(h_ref[...], w_ref[...], preferred_element_type=F32)
    pos = (pl.program_id(0) * tm) % seq + lax.broadcasted_iota(jnp.int32, (tm, LANES), 0)
    lane = lax.broadcasted_iota(jnp.int32, (tm, LANES), 1)
    onehot = jnp.where(lane * BLOCK == pos - pos % BLOCK, 1.0, 0.0).astype(BF16)
    for hh in range(heads_per_tile):
        o_ref[hh, :, :HEAD_DIM] = _head_norm(res[:, hh * HEAD_DIM:(hh + 1) * HEAD_DIM], nw_ref[...]).astype(BF16)
        o_ref[hh, :, HEAD_DIM:] = onehot


def _store_transposed(blk, o_ref, hh):
    blk_t = blk.T.astype(BF16)
    for pp in range(o_ref.shape[1]):
        o_ref[hh, pp, :HEAD_DIM] = blk_t[:, pp * PAIR:(pp + 1) * PAIR]


def _qt_kernel(h_ref, w_ref, nw_ref, o_ref, *, heads_per_tile):
    res = jnp.dot(h_ref[...], w_ref[...], preferred_element_type=F32)
    for hh in range(heads_per_tile):
        blk = _head_norm(res[:, hh * HEAD_DIM:(hh + 1) * HEAD_DIM], nw_ref[...])
        _store_transposed(blk * QK_SCALE_LOG2, o_ref, hh)


def _vt_kernel(h_ref, w_ref, o_ref, *, heads_per_tile):
    res = jnp.dot(h_ref[...], w_ref[...], preferred_element_type=F32)
    for hh in range(heads_per_tile):
        _store_transposed(res[:, hh * HEAD_DIM:(hh + 1) * HEAD_DIM], o_ref, hh)
        o_ref[hh, :, HEAD_DIM:] = jnp.ones((o_ref.shape[1], V_ROWS - HEAD_DIM, PAIR), BF16)


def _head_proj(h, w, col0, n, norm_w, seq, kind):
    m, d = h.shape
    tm = _pick(seq, 1024)
    tn = _pick(n, 1024)
    assert col0 % tn == 0
    hpt = tn // HEAD_DIM
    in_specs = [pl.BlockSpec((tm, d), lambda i, j: (i, 0)),
                pl.BlockSpec((d, tn), lambda i, j: (0, col0 // tn + j))]
    args = [h, w]
    if kind != "v":
        in_specs.append(pl.BlockSpec((1, HEAD_DIM), lambda i, j: (0, 0)))
        args.append(norm_w)
    if kind == "k":
        body = functools.partial(_k_kernel, heads_per_tile=hpt, seq=seq)
        out_spec = pl.BlockSpec((hpt, tm, 2 * HEAD_DIM), lambda i, j: (j, i, 0))
        out_shape = (n // HEAD_DIM, m, 2 * HEAD_DIM)
    else:
        body = functools.partial(_qt_kernel if kind == "q" else _vt_kernel, heads_per_tile=hpt)
        rows = HEAD_DIM if kind == "q" else V_ROWS
        out_spec = pl.BlockSpec((hpt, tm // PAIR, rows, PAIR), lambda i, j: (j, i, 0, 0))
        out_shape = (n // HEAD_DIM, m // PAIR, rows, PAIR)
    return pl.pallas_call(
        body,
        grid=(m // tm, n // tn),
        in_specs=in_specs,
        out_specs=out_spec,
        out_shape=jax.ShapeDtypeStruct(out_shape, BF16),
        compiler_params=_params("parallel", "arbitrary"),
        name=f"{kind}_proj",
    )(*args)


def _conv_kernel(h_ref, wc_ref, wb_ref, wh_ref, cw_ref, *refs, tiles_per_batch, n_cast):
    cast_in, o_ref, cast_out, ubuf_ref = refs[:n_cast], refs[n_cast], refs[n_cast + 1:2 * n_cast + 1], refs[-1]
    _cast_streams(cast_in, cast_out)
    i = pl.program_id(1)
    tm = h_ref.shape[0]

    @pl.when(i % tiles_per_batch == 0)
    def _():
        ubuf_ref[0:SUBLANES, :] = jnp.zeros((SUBLANES, ubuf_ref.shape[1]), F32)

    h = h_ref[...]
    cg = jnp.dot(h, wc_ref[...], preferred_element_type=F32)
    hc = jnp.dot(h, wh_ref[...], preferred_element_type=F32)
    ubuf_ref[SUBLANES:SUBLANES + tm, :] = cg * hc
    u0 = ubuf_ref[SUBLANES:SUBLANES + tm, :]
    u1 = ubuf_ref[SUBLANES - 1:SUBLANES - 1 + tm, :]
    u2 = ubuf_ref[SUBLANES - 2:SUBLANES - 2 + tm, :]
    y = cw_ref[0:1, :] * u2 + cw_ref[1:2, :] * u1 + cw_ref[2:3, :] * u0
    bg = jnp.dot(h, wb_ref[...], preferred_element_type=F32)
    o_ref[...] = (bg * y).astype(BF16)
    ubuf_ref[0:SUBLANES, :] = ubuf_ref[tm:tm + SUBLANES, :]


def _conv(h, w, col0, cw, conv_w, seq, to_cast):
    m, d = h.shape
    tm = _pick(seq, 1024)
    tc = _pick(cw, 256)
    assert col0 % tc == 0
    n_i = m // tm
    wspec = lambda group: pl.BlockSpec((d, tc), lambda c, i: (0, (col0 + group * cw) // tc + c))
    cast_in, cast_out, cast_shapes = _cast_plan(to_cast, (cw // tc) * n_i, lambda c, i: c * n_i + i)
    outs = pl.pallas_call(
        functools.partial(_conv_kernel, tiles_per_batch=seq // tm, n_cast=len(to_cast)),
        grid=(cw // tc, n_i),
        in_specs=[pl.BlockSpec((tm, d), lambda c, i: (i, 0)), wspec(0), wspec(1), wspec(2),
                  pl.BlockSpec((CONV_WIDTH, tc), lambda c, i: (0, c))] + cast_in,
        out_specs=[pl.BlockSpec((tm, tc), lambda c, i: (i, c))] + cast_out,
        out_shape=[jax.ShapeDtypeStruct((m, cw), BF16)] + cast_shapes,
        scratch_shapes=[pltpu.VMEM((tm + SUBLANES, tc), F32)],
        compiler_params=_params("arbitrary", "arbitrary"),
        name="conv_proj",
    )(h, w, w, w, conv_w, *[a for a, _ in to_cast])
    return outs[0], outs[1:]


def _t5_bucket_np(dist):
    max_exact = N_BUCKETS // 2
    d = np.maximum(dist, 1).astype(np.float32)
    large = max_exact + (np.log(d / np.float32(max_exact)) / np.float32(math.log(MAX_DISTANCE / max_exact))
                         * np.float32(N_BUCKETS - max_exact)).astype(np.int32)
    large = np.minimum(large, N_BUCKETS - 1)
    return np.where(dist < max_exact, dist, large)


def _bias_plan(seq):
    buckets = _t5_bucket_np(np.arange(seq, dtype=np.int64))
    assert (np.diff(buckets) >= 0).all()
    thr = [int(np.argmax(buckets >= b)) if (buckets >= b).any() else seq for b in range(N_BUCKETS)]
    nb = seq // BLOCK
    n_tiles = nb
    for delta in range(1, nb):
        if buckets[delta * BLOCK - (BLOCK - 1)] == buckets[-1]:
            n_tiles = delta + 1
            break
    ranges = []
    for delta in range(n_tiles):
        lo = max(0, delta * BLOCK - (BLOCK - 1))
        hi = min(seq - 1, delta * BLOCK + (BLOCK - 1))
        ranges.append((int(buckets[lo]), int(buckets[hi])))
    return thr, ranges


def _work_items(nb):
    items = [(u, jj) for u in range(nb // 2) for jj in range(u + 1)]
    tab = np.zeros((3, len(items)), np.int32)
    for t, (u, jj) in enumerate(items):
        tab[0, t], tab[1, t] = u, jj
        tab[2, t] = 1 if t >= 1 and items[t - 1][1] == items[t - 1][0] else 0
    return tab


def _attn_kernel(tbl_ref, item_ref, qt_ref, k_ref, vt_ref, o_ref, bias_ref, kmean_ref, qa_ref,
                 logit0_ref, logit1_ref, p0_ref, p1_ref, *, nb, thr, ranges, n_steps):
    hd = pl.program_id(0)
    b = pl.program_id(1)
    n_tiles = len(ranges)
    n_sb = nb // 2

    @pl.when(b == 0)
    def _build_bias():
        rel = (lax.broadcasted_iota(jnp.int32, (BLOCK, BLOCK), 1)
               - lax.broadcasted_iota(jnp.int32, (BLOCK, BLOCK), 0))
        for delta, (b_lo, b_hi) in enumerate(ranges):
            dist = rel + delta * BLOCK
            val = jnp.full((BLOCK, BLOCK), tbl_ref[hd, b_lo], F32)
            for bk in range(b_lo + 1, b_hi + 1):
                val = jnp.where(dist >= thr[bk], tbl_ref[hd, bk], val)
            val = val * LOG2E
            if delta == 0:
                val = jnp.where(rel >= 0, val, NEG)
            bias_ref[delta] = val

    for j in range(nb):
        kj = k_ref[j * BLOCK:(j + 1) * BLOCK, :HEAD_DIM].astype(F32)
        kmean_ref[j:j + 1, :] = jnp.sum(kj, axis=0, keepdims=True) * (1.0 / BLOCK)
    kmean = kmean_ref[...].astype(BF16)

    blk_id = lax.broadcasted_iota(jnp.int32, (nb, PAIR), 0)
    q_half = (lax.broadcasted_iota(jnp.int32, (1, PAIR), 1) >= BLOCK).astype(jnp.int32)

    def build_qa(u, carry):
        q_t = qt_ref[u]
        q_blk = 2 * u + q_half
        gate = jnp.dot(kmean, q_t, preferred_element_type=F32)
        rank = jnp.zeros((nb, PAIR), jnp.int32)
        for jp in range(nb):
            live = (q_blk > jp).astype(jnp.int32)
            tie = jnp.where(blk_id > jp, live, 0)
            g_jp = gate[jp:jp + 1, :]
            rank = rank + jnp.where(g_jp > gate, live, jnp.where(g_jp == gate, tie, 0))
        keep = jnp.where(blk_id < q_blk, rank, jnp.where(blk_id == q_blk, 0, TOPK)) < TOPK
        qa_ref[u, :HEAD_DIM] = q_t
        qa_ref[u, HEAD_DIM:HEAD_DIM + nb] = jnp.where(keep, 0.0, NEG).astype(BF16)
        return carry

    qa_ref[:, HEAD_DIM + nb:] = jnp.zeros((n_sb, LANES - nb, PAIR), BF16)
    lax.fori_loop(0, n_sb, build_qa, 0)

    def qk_logits(t, logit_ref):
        u = item_ref[0, t]
        jj = item_ref[1, t]
        r0 = pl.multiple_of(jj * PAIR, PAIR)
        raw = jnp.dot(k_ref[pl.ds(r0, PAIR), :], qa_ref[u], preferred_element_type=F32)
        d0 = 2 * (u - jj)
        tile = lambda d: bias_ref[jnp.clip(d, 0, n_tiles - 1)]
        top = raw[:BLOCK] + jnp.concatenate([tile(d0), tile(d0 + 1)], axis=1)
        bot = raw[BLOCK:] + jnp.concatenate([tile(d0 - 1), tile(d0)], axis=1)
        logit_ref[:BLOCK] = top
        logit_ref[BLOCK:] = bot
        return jnp.maximum(jnp.max(top, axis=0, keepdims=True), jnp.max(bot, axis=0, keepdims=True))

    def pv(t_prev, alpha_prev, acc, p_prev):
        return alpha_prev * acc + jnp.dot(vt_ref[item_ref[1, t_prev]], p_prev[...], preferred_element_type=F32)

    def step(t, c, logit_cur, logit_next, p_cur, p_prev):
        cmax, alpha_prev, m, acc = c
        cmax_next = qk_logits(jnp.minimum(t + 1, n_steps - 1), logit_next)
        acc = pv(jnp.maximum(t - 1, 0), alpha_prev, acc, p_prev)

        m_old = jnp.where(item_ref[1, t] == 0, NEG, m)
        m_new = jnp.maximum(m_old, cmax)
        alpha = jnp.exp2(m_old - m_new)
        p_cur[...] = jnp.exp2(logit_cur[...] - m_new).astype(BF16)
        return (cmax_next, alpha, m_new, acc), (t, acc)

    def store(u, acc):
        r0 = pl.multiple_of(u * PAIR, PAIR)
        o_ref[pl.ds(r0, PAIR), :] = (acc[:HEAD_DIM] * (1.0 / acc[HEAD_DIM:HEAD_DIM + 1])).T.astype(BF16)

    def finish(done):
        t, acc = done
        pl.when(item_ref[2, t] == 1)(lambda: store(item_ref[0, jnp.maximum(t - 1, 0)], acc))

    logit_refs = (logit0_ref, logit1_ref)
    p_refs = (p0_ref, p1_ref)

    def steps(t0, count, c):
        done = []
        for s in range(count):
            c, d = step(t0 + s, c, logit_refs[s % 2], logit_refs[1 - s % 2], p_refs[s % 2], p_refs[1 - s % 2])
            done.append(d)
        for d in done:
            finish(d)
        return c

    p1_ref[...] = jnp.zeros_like(p1_ref)
    c = (qk_logits(0, logit0_ref), jnp.zeros((1, PAIR), F32),
         jnp.full((1, PAIR), NEG, F32), jnp.zeros((V_ROWS, PAIR), F32))
    n_main = n_steps // ATTN_UNROLL
    c = lax.fori_loop(0, n_main, lambda tt, c: steps(ATTN_UNROLL * tt, ATTN_UNROLL, c), c)
    _, alpha_prev, _, acc = steps(n_main * ATTN_UNROLL, n_steps % ATTN_UNROLL, c)
    store(n_sb - 1, pv(n_steps - 1, alpha_prev, acc, p_refs[(n_steps - 1) % 2]))


def _attention(qt, k, vt, rel_bias_t, batch, seq, n_heads):
    nb = seq // BLOCK
    assert nb % 2 == 0 and nb <= LANES
    thr, ranges = _bias_plan(seq)
    items = _work_items(nb)
    n_steps = items.shape[1]
    m = batch * seq
    t_spec = lambda rows: pl.BlockSpec((None, seq // PAIR, rows, PAIR), lambda h, b: (h, b, 0, 0))
    return pl.pallas_call(
        functools.partial(_attn_kernel, nb=nb, thr=thr, ranges=ranges, n_steps=n_steps),
        grid=(n_heads, batch),
        in_specs=[pl.BlockSpec(memory_space=pltpu.SMEM),
                  pl.BlockSpec(memory_space=pltpu.SMEM),
                  t_spec(HEAD_DIM),
                  pl.BlockSpec((None, seq, 2 * HEAD_DIM), lambda h, b: (h, b, 0)),
                  t_spec(V_ROWS)],
        out_specs=pl.BlockSpec((seq, HEAD_DIM), lambda h, b: (b, h)),
        out_shape=jax.ShapeDtypeStruct((m, n_heads * HEAD_DIM), BF16),
        scratch_shapes=[pltpu.VMEM((len(ranges), BLOCK, BLOCK), F32),
                        pltpu.VMEM((nb, HEAD_DIM), F32),
                        pltpu.VMEM((nb // 2, 2 * LANES, PAIR), BF16),
                        pltpu.VMEM((PAIR, PAIR), F32),
                        pltpu.VMEM((PAIR, PAIR), F32),
                        pltpu.VMEM((PAIR, PAIR), BF16),
                        pltpu.VMEM((PAIR, PAIR), BF16)],
        compiler_params=_params("arbitrary", "arbitrary"),
        name="moba_attn",
    )(rel_bias_t, jnp.asarray(items), qt, k, vt)


def _outproj_kernel(attn_ref, conv_ref, wa_ref, wc_ref, w_ref, x_ref, ada_ref, o_ref, y_ref, *, k):
    j = pl.program_id(1)
    aw = attn_ref.shape[1]

    @pl.when(j == 0)
    def _():
        a = attn_ref[...].astype(F32)
        y_ref[:, :aw] = (a * lax.rsqrt(jnp.mean(a * a, axis=-1, keepdims=True) + EPS) * wa_ref[...]).astype(BF16)
        c = conv_ref[...].astype(F32)
        y_ref[:, aw:] = (c * lax.rsqrt(jnp.mean(c * c, axis=-1, keepdims=True) + EPS) * wc_ref[...]).astype(BF16)

    res = jnp.dot(y_ref[...], w_ref[...], preferred_element_type=F32)
    o_ref[...] = x_ref[...] + ada_ref[3 * k + 2:3 * k + 3, :] * res


def _outproj(attn, conv, wa, wc, w_out, x2d, ada, k, seq):
    m, aw = attn.shape
    cw = conv.shape[1]
    d = w_out.shape[1]
    tm = _pick(seq, 512)
    tn = _pick(d, 1024)
    return pl.pallas_call(
        functools.partial(_outproj_kernel, k=k),
        grid=(m // tm, d // tn),
        in_specs=[pl.BlockSpec((tm, aw), lambda i, j: (i, 0)),
                  pl.BlockSpec((tm, cw), lambda i, j: (i, 0)),
                  pl.BlockSpec((1, aw), lambda i, j: (0, 0)),
                  pl.BlockSpec((1, cw), lambda i, j: (0, 0)),
                  pl.BlockSpec((aw + cw, tn), lambda i, j: (0, j)),
                  pl.BlockSpec((tm, tn), lambda i, j: (i, j)),
                  pl.BlockSpec((None, N_ADA, tn), lambda i, j: (i * tm // seq, 0, j))],
        out_specs=pl.BlockSpec((tm, tn), lambda i, j: (i, j)),
        out_shape=jax.ShapeDtypeStruct((m, d), F32),
        scratch_shapes=[pltpu.VMEM((tm, aw + cw), BF16)],
        compiler_params=_params("parallel", "arbitrary"),
        name="out_proj",
    )(attn, conv, wa, wc, w_out, x2d, ada)


def _pad_to(a, axis, mult):
    pad = -a.shape[axis] % mult
    if not pad:
        return a
    widths = [(0, 0)] * a.ndim
    widths[axis] = (0, pad)
    return jnp.pad(a, widths)


def kernel(x, c, ada_w, ada_b, ffn1_norm, ffn1_w_gate, ffn1_w_up, ffn1_w_down, mix_norm, w_in, q_norm, k_norm,
           rel_bias, conv_w, attn_out_norm, conv_out_norm, w_out, ffn2_norm, ffn2_w_gate, ffn2_w_up, ffn2_w_down):
    batch, seq, d = x.shape
    depth = ada_w.shape[0]
    mix_w = w_out.shape[1]
    att_w = mix_w // 2
    conv_cw = mix_w - att_w
    n_heads = att_w // HEAD_DIM
    assert seq % BLOCK == 0 and att_w % HEAD_DIM == 0

    xs = x.reshape(batch * seq, d)
    c_pad = _pad_to(c, 0, SUBLANES)
    rel_bias_t = rel_bias.T
    for l in range(depth):
        ada = _ada(c_pad, ada_w[l], ada_b[l][None, :])[:batch].reshape(batch, N_ADA, d)

        h = _norm_mod(xs, ffn1_norm[l][None, :], ada, 0, seq)
        xs, (w_in_l,) = _ffn(h, ffn1_w_gate[l].astype(BF16), ffn1_w_up[l].astype(BF16), ffn1_w_down[l].astype(BF16),
                             xs, ada, 0, seq, to_cast=[(w_in[l], None)])

        h = _norm_mod(xs, mix_norm[l][None, :], ada, 1, seq)
        qt = _head_proj(h, w_in_l, 0, att_w, q_norm[l][None, :], seq, "q")
        kk = _head_proj(h, w_in_l, att_w, att_w, k_norm[l][None, :], seq, "k")
        vt = _head_proj(h, w_in_l, 2 * att_w, att_w, None, seq, "v")
        conv, (wg2, wu2, wd2, w_out_l) = _conv(h, w_in_l, 3 * att_w, conv_cw, conv_w[l], seq,
                                               [(ffn2_w_gate[l], FFN_TILE), (ffn2_w_up[l], FFN_TILE),
                                                (ffn2_w_down[l], None), (w_out[l], None)])
        attn = _attention(qt, kk, vt, rel_bias_t, batch, seq, n_heads)
        xs = _outproj(attn, conv, attn_out_norm[l][None, :], conv_out_norm[l][None, :], w_out_l, xs, ada, 1, seq)

        h = _norm_mod(xs, ffn2_norm[l][None, :], ada, 2, seq)
        xs, _ = _ffn(h, wg2, wu2, wd2, xs, ada, 2, seq)
    return xs.reshape(batch, seq, d)
```

```python
import functools
import math

import numpy as np
import jax
import jax.numpy as jnp
from jax import lax
from jax.experimental import pallas as pl
from jax.experimental.pallas import tpu as pltpu

HEAD_DIM = 128
BLOCK = 256
TOPK = 3
CONV_WIDTH = 3
N_BUCKETS = 32
MAX_DISTANCE = 2048
N_ADA = 9
EPS = 1e-6
NEG = -1e30
PAIR = 2 * BLOCK
ATTN_UNROLL = 2
V_ROWS = HEAD_DIM + 16
FFN_TILE = 512
LOG2E = 1.4426950408889634
QK_SCALE_LOG2 = HEAD_DIM ** -0.5 * LOG2E

LANES = 128
SUBLANES = 8
VMEM_LIMIT_BYTES = 60000 * 1024

F32 = jnp.float32
BF16 = jnp.bfloat16


def _params(*sem):
    return pltpu.CompilerParams(dimension_semantics=sem, vmem_limit_bytes=VMEM_LIMIT_BYTES)


def _pick(n, pref):
    if n <= pref:
        return n
    t = pref
    while n % t:
        t //= 2
    return t


def _cast_plan(to_cast, n_steps, step_of):
    in_specs, out_specs, out_shapes = [], [], []
    for a, tile in to_cast:
        rows = -(-a.shape[0] // n_steps)
        rows = -(-rows // (2 * SUBLANES)) * 2 * SUBLANES
        while a.shape[0] % rows:
            rows += 2 * SUBLANES
        block = lambda *g, last=a.shape[0] // rows - 1: jnp.minimum(step_of(*g), last)
        in_specs.append(pl.BlockSpec((rows, a.shape[1]), lambda *g, block=block: (block(*g), 0)))
        if tile is None:
            out_specs.append(in_specs[-1])
            out_shapes.append(jax.ShapeDtypeStruct(a.shape, BF16))
        else:
            n_tiles = -(-a.shape[1] // tile)
            out_specs.append(pl.BlockSpec((n_tiles, rows, tile), lambda *g, block=block: (0, block(*g), 0)))
            out_shapes.append(jax.ShapeDtypeStruct((n_tiles, a.shape[0], tile), BF16))
    return in_specs, out_specs, out_shapes


def _cast_streams(cast_in, cast_out):
    for src, dst in zip(cast_in, cast_out):
        if len(dst.shape) == 2:
            dst[...] = src[...].astype(BF16)
            continue
        n_tiles, rows, tile = dst.shape
        for t in range(n_tiles):
            width = min(tile, src.shape[1] - t * tile)
            dst[t, :, :width] = src[:, t * tile:t * tile + width].astype(BF16)
            if width < tile:
                dst[t, :, width:] = jnp.zeros((rows, tile - width), BF16)


def _ada_kernel(c_ref, w_ref, b_ref, o_ref):
    c = c_ref[...]
    cond = (c * jax.nn.sigmoid(c)).astype(BF16)
    o_ref[...] = jnp.dot(cond, w_ref[...].astype(BF16), preferred_element_type=F32) + b_ref[...]


def _ada(c_pad, ada_w, ada_b):
    rows, d = c_pad.shape
    n = ada_w.shape[1]
    tn = _pick(n, 512)
    return pl.pallas_call(
        _ada_kernel,
        grid=(n // tn,),
        in_specs=[pl.BlockSpec((rows, d), lambda j: (0, 0)),
                  pl.BlockSpec((d, tn), lambda j: (0, j)),
                  pl.BlockSpec((1, tn), lambda j: (0, j))],
        out_specs=pl.BlockSpec((rows, tn), lambda j: (0, j)),
        out_shape=jax.ShapeDtypeStruct((rows, n), F32),
        compiler_params=_params("parallel"),
        name="ada_proj",
    )(c_pad, ada_w, ada_b)


def _norm_mod_kernel(x_ref, w_ref, ada_ref, o_ref, *, k):
    x = x_ref[...]
    ms = jnp.mean(x * x, axis=-1, keepdims=True)
    y = x * lax.rsqrt(ms + EPS) * w_ref[...]
    shift = ada_ref[3 * k:3 * k + 1, :]
    scale = ada_ref[3 * k + 1:3 * k + 2, :]
    o_ref[...] = (y * (1.0 + scale) + shift).astype(BF16)


def _norm_mod(x2d, w, ada, k, seq):
    m, d = x2d.shape
    tr = _pick(seq, 512)
    return pl.pallas_call(
        functools.partial(_norm_mod_kernel, k=k),
        grid=(m // tr,),
        in_specs=[pl.BlockSpec((tr, d), lambda i: (i, 0)),
                  pl.BlockSpec((1, d), lambda i: (0, 0)),
                  pl.BlockSpec((None, N_ADA, d), lambda i: (i * tr // seq, 0, 0))],
        out_specs=pl.BlockSpec((tr, d), lambda i: (i, 0)),
        out_shape=jax.ShapeDtypeStruct((m, d), BF16),
        compiler_params=_params("parallel"),
        name=f"norm_mod{k}",
    )(x2d, w, ada)


def _ffn_kernel(h_ref, wg_ref, wu_ref, wd_ref, x_ref, ada_ref, *refs, k, n_xchunks, xc, nchunk, tail, n_cast):
    cast_in, o_ref, cast_out = refs[:n_cast], refs[n_cast], refs[n_cast + 1:]
    _cast_streams(cast_in, cast_out)
    f = pl.program_id(1)
    nf = pl.num_programs(1)
    d = o_ref.shape[1]
    tf = wg_ref.shape[1]

    @pl.when(f == 0)
    def _():
        o_ref[...] = jnp.zeros_like(o_ref)

    for c in range(n_xchunks):
        @pl.when(f == c)
        def _(c=c):
            o_ref[:, c * xc:(c + 1) * xc] += x_ref[...]

    def accumulate(width):
        h = h_ref[...]
        g = jnp.dot(h, wg_ref[:, :width], preferred_element_type=F32)
        u = jnp.dot(h, wu_ref[:, :width], preferred_element_type=F32)
        a = (g * jax.nn.sigmoid(g) * u).astype(BF16)
        half_gate = 0.5 * ada_ref[3 * k + 2:3 * k + 3, :]
        for n0 in range(0, d, nchunk):
            part = jnp.dot(a, wd_ref[:width, n0:n0 + nchunk], preferred_element_type=F32)
            o_ref[:, n0:n0 + nchunk] += half_gate[:, n0:n0 + nchunk] * part

    if tail == tf:
        accumulate(tf)
    else:
        pl.when(f < nf - 1)(lambda: accumulate(tf))
        pl.when(f == nf - 1)(lambda: accumulate(tail))


def _ffn(h, wg, wu, wd, x2d, ada, k, seq, to_cast=(), tm_pref=512):
    m, d = h.shape
    d_ff = wd.shape[0]
    tm = _pick(seq, tm_pref)
    tf = min(FFN_TILE, d_ff)
    nf = pl.cdiv(d_ff, tf)
    tail = d_ff - (nf - 1) * tf
    assert tail % LANES == 0, (d_ff, tf)
    n_xchunks = 1
    while d % n_xchunks or (d // n_xchunks) % LANES or d // n_xchunks > 512:
        n_xchunks += 1
    assert n_xchunks <= nf, (n_xchunks, nf)
    xc = d // n_xchunks
    nchunk = _pick(d, 512)
    cast_in, cast_out, cast_shapes = _cast_plan(to_cast, (m // tm) * nf, lambda i, f: i * nf + f)
    if wg.ndim == 3:
        assert wg.shape == wu.shape == (nf, d, tf), (wg.shape, wu.shape)
        w_spec = pl.BlockSpec((None, d, tf), lambda i, f: (f, 0, 0))
    else:
        w_spec = pl.BlockSpec((d, tf), lambda i, f: (0, f))
    outs = pl.pallas_call(
        functools.partial(_ffn_kernel, k=k, n_xchunks=n_xchunks, xc=xc, nchunk=nchunk, tail=tail,
                          n_cast=len(to_cast)),
        grid=(m // tm, nf),
        in_specs=[pl.BlockSpec((tm, d), lambda i, f: (i, 0)),
                  w_spec,
                  w_spec,
                  pl.BlockSpec((tf, d), lambda i, f: (f, 0)),
                  pl.BlockSpec((tm, xc), lambda i, f: (i, jnp.minimum(f, n_xchunks - 1))),
                  pl.BlockSpec((None, N_ADA, d), lambda i, f: (i * tm // seq, 0, 0))] + cast_in,
        out_specs=[pl.BlockSpec((tm, d), lambda i, f: (i, 0))] + cast_out,
        out_shape=[jax.ShapeDtypeStruct((m, d), F32)] + cast_shapes,
        compiler_params=_params("arbitrary", "arbitrary"),
        name=f"ffn{k}",
    )(h, wg, wu, wd, x2d, ada, *[a for a, _ in to_cast])
    return outs[0], outs[1:]


def _head_norm(blk, w):
    ms = jnp.mean(blk * blk, axis=-1, keepdims=True)
    return blk * lax.rsqrt(ms + EPS) * w


def _k_kernel(h_ref, w_ref, nw_ref, o_ref, *, heads_per_tile, seq):
    tm = h_ref.shape[0]
    res = jnp.dot(h_ref[...], w_ref[...], preferred_element_type=F32)
    pos = (pl.program_id(0) * tm) % seq + lax.broadcasted_iota(jnp.int32, (tm, LANES), 0)
    lane = lax.broadcasted_iota(jnp.int32, (tm, LANES), 1)
    onehot = jnp.where(lane * BLOCK == pos - pos % BLOCK, 1.0, 0.0).astype(BF16)
    for hh in range(heads_per_tile):
        o_ref[hh, :, :HEAD_DIM] = _head_norm(res[:, hh * HEAD_DIM:(hh + 1) * HEAD_DIM], nw_ref[...]).astype(BF16)
        o_ref[hh, :, HEAD_DIM:] = onehot


def _store_transposed(blk, o_ref, hh):
    blk_t = blk.T.astype(BF16)
    for pp in range(o_ref.shape[1]):
        o_ref[hh, pp, :HEAD_DIM] = blk_t[:, pp * PAIR:(pp + 1) * PAIR]


def _qt_kernel(h_ref, w_ref, nw_ref, o_ref, *, heads_per_tile):
    res = jnp.dot(h_ref[...], w_ref[...], preferred_element_type=F32)
    for hh in range(heads_per_tile):
        blk = _head_norm(res[:, hh * HEAD_DIM:(hh + 1) * HEAD_DIM], nw_ref[...])
        _store_transposed(blk * QK_SCALE_LOG2, o_ref, hh)


def _vt_kernel(h_ref, w_ref, o_ref, *, heads_per_tile):
    res = jnp.dot(h_ref[...], w_ref[...], preferred_element_type=F32)
    for hh in range(heads_per_tile):
        _store_transposed(res[:, hh * HEAD_DIM:(hh + 1) * HEAD_DIM], o_ref, hh)
        o_ref[hh, :, HEAD_DIM:] = jnp.ones((o_ref.shape[1], V_ROWS - HEAD_DIM, PAIR), BF16)


def _head_proj(h, w, col0, n, norm_w, seq, kind):
    m, d = h.shape
    tm = _pick(seq, 1024)
    tn = _pick(n, 1024)
    assert col0 % tn == 0
    hpt = tn // HEAD_DIM
    in_specs = [pl.BlockSpec((tm, d), lambda i, j: (i, 0)),
                pl.BlockSpec((d, tn), lambda i, j: (0, col0 // tn + j))]
    args = [h, w]
    if kind != "v":
        in_specs.append(pl.BlockSpec((1, HEAD_DIM), lambda i, j: (0, 0)))
        args.append(norm_w)
    if kind == "k":
        body = functools.partial(_k_kernel, heads_per_tile=hpt, seq=seq)
        out_spec = pl.BlockSpec((hpt, tm, 2 * HEAD_DIM), lambda i, j: (j, i, 0))
        out_shape = (n // HEAD_DIM, m, 2 * HEAD_DIM)
    else:
        body = functools.partial(_qt_kernel if kind == "q" else _vt_kernel, heads_per_tile=hpt)
        rows = HEAD_DIM if kind == "q" else V_ROWS
        out_spec = pl.BlockSpec((hpt, tm // PAIR, rows, PAIR), lambda i, j: (j, i, 0, 0))
        out_shape = (n // HEAD_DIM, m // PAIR, rows, PAIR)
    return pl.pallas_call(
        body,
        grid=(m // tm, n // tn),
        in_specs=in_specs,
        out_specs=out_spec,
        out_shape=jax.ShapeDtypeStruct(out_shape, BF16),
        compiler_params=_params("parallel", "arbitrary"),
        name=f"{kind}_proj",
    )(*args)


def _conv_kernel(h_ref, wc_ref, wb_ref, wh_ref, cw_ref, *refs, tiles_per_batch, n_cast):
    cast_in, o_ref, cast_out, ubuf_ref = refs[:n_cast], refs[n_cast], refs[n_cast + 1:2 * n_cast + 1], refs[-1]
    _cast_streams(cast_in, cast_out)
    i = pl.program_id(1)
    tm = h_ref.shape[0]

    @pl.when(i % tiles_per_batch == 0)
    def _():
        ubuf_ref[0:SUBLANES, :] = jnp.zeros((SUBLANES, ubuf_ref.shape[1]), F32)

    h = h_ref[...]
    cg = jnp.dot(h, wc_ref[...], preferred_element_type=F32)
    hc = jnp.dot(h, wh_ref[...], preferred_element_type=F32)
    ubuf_ref[SUBLANES:SUBLANES + tm, :] = cg * hc
    u0 = ubuf_ref[SUBLANES:SUBLANES + tm, :]
    u1 = ubuf_ref[SUBLANES - 1:SUBLANES - 1 + tm, :]
    u2 = ubuf_ref[SUBLANES - 2:SUBLANES - 2 + tm, :]
    y = cw_ref[0:1, :] * u2 + cw_ref[1:2, :] * u1 + cw_ref[2:3, :] * u0
    bg = jnp.dot(h, wb_ref[...], preferred_element_type=F32)
    o_ref[...] = (bg * y).astype(BF16)
    ubuf_ref[0:SUBLANES, :] = ubuf_ref[tm:tm + SUBLANES, :]


def _conv(h, w, col0, cw, conv_w, seq, to_cast):
    m, d = h.shape
    tm = _pick(seq, 1024)
    tc = _pick(cw, 256)
    assert col0 % tc == 0
    n_i = m // tm
    wspec = lambda group: pl.BlockSpec((d, tc), lambda c, i: (0, (col0 + group * cw) // tc + c))
    cast_in, cast_out, cast_shapes = _cast_plan(to_cast, (cw // tc) * n_i, lambda c, i: c * n_i + i)
    outs = pl.pallas_call(
        functools.partial(_conv_kernel, tiles_per_batch=seq // tm, n_cast=len(to_cast)),
        grid=(cw // tc, n_i),
        in_specs=[pl.BlockSpec((tm, d), lambda c, i: (i, 0)), wspec(0), wspec(1), wspec(2),
                  pl.BlockSpec((CONV_WIDTH, tc), lambda c, i: (0, c))] + cast_in,
        out_specs=[pl.BlockSpec((tm, tc), lambda c, i: (i, c))] + cast_out,
        out_shape=[jax.ShapeDtypeStruct((m, cw), BF16)] + cast_shapes,
        scratch_shapes=[pltpu.VMEM((tm + SUBLANES, tc), F32)],
        compiler_params=_params("arbitrary", "arbitrary"),
        name="conv_proj",
    )(h, w, w, w, conv_w, *[a for a, _ in to_cast])
    return outs[0], outs[1:]


def _t5_bucket_np(dist):
    max_exact = N_BUCKETS // 2
    d = np.maximum(dist, 1).astype(np.float32)
    large = max_exact + (np.log(d / np.float32(max_exact)) / np.float32(math.log(MAX_DISTANCE / max_exact))
                         * np.float32(N_BUCKETS - max_exact)).astype(np.int32)
    large = np.minimum(large, N_BUCKETS - 1)
    return np.where(dist < max_exact, dist, large)


def _bias_plan(seq):
    buckets = _t5_bucket_np(np.arange(seq, dtype=np.int64))
    assert (np.diff(buckets) >= 0).all()
    thr = [int(np.argmax(buckets >= b)) if (buckets >= b).any() else seq for b in range(N_BUCKETS)]
    nb = seq // BLOCK
    n_tiles = nb
    for delta in range(1, nb):
        if buckets[delta * BLOCK - (BLOCK - 1)] == buckets[-1]:
            n_tiles = delta + 1
            break
    ranges = []
    for delta in range(n_tiles):
        lo = max(0, delta * BLOCK - (BLOCK - 1))
        hi = min(seq - 1, delta * BLOCK + (BLOCK - 1))
        ranges.append((int(buckets[lo]), int(buckets[hi])))
    return thr, ranges


def _work_items(nb):
    items = [(u, jj) for u in range(nb // 2) for jj in range(u + 1)]
    tab = np.zeros((3, len(items)), np.int32)
    for t, (u, jj) in enumerate(items):
        tab[0, t], tab[1, t] = u, jj
        tab[2, t] = 1 if t >= 1 and items[t - 1][1] == items[t - 1][0] else 0
    return tab


def _attn_kernel(tbl_ref, item_ref, qt_ref, k_ref, vt_ref, o_ref, bias_ref, kmean_ref, qa_ref,
                 logit0_ref, logit1_ref, p0_ref, p1_ref, *, nb, thr, ranges, n_steps):
    hd = pl.program_id(0)
    b = pl.program_id(1)
    n_tiles = len(ranges)
    n_sb = nb // 2

    @pl.when(b == 0)
    def _build_bias():
        rel = (lax.broadcasted_iota(jnp.int32, (BLOCK, BLOCK), 1)
               - lax.broadcasted_iota(jnp.int32, (BLOCK, BLOCK), 0))
        for delta, (b_lo, b_hi) in enumerate(ranges):
            dist = rel + delta * BLOCK
            val = jnp.full((BLOCK, BLOCK), tbl_ref[hd, b_lo], F32)
            for bk in range(b_lo + 1, b_hi + 1):
                val = jnp.where(dist >= thr[bk], tbl_ref[hd, bk], val)
            val = val * LOG2E
            if delta == 0:
                val = jnp.where(rel >= 0, val, NEG)
            bias_ref[delta] = val

    for j in range(nb):
        kj = k_ref[j * BLOCK:(j + 1) * BLOCK, :HEAD_DIM].astype(F32)
        kmean_ref[j:j + 1, :] = jnp.sum(kj, axis=0, keepdims=True) * (1.0 / BLOCK)
    kmean = kmean_ref[...].astype(BF16)

    blk_id = lax.broadcasted_iota(jnp.int32, (nb, PAIR), 0)
    q_half = (lax.broadcasted_iota(jnp.int32, (1, PAIR), 1) >= BLOCK).astype(jnp.int32)

    def build_qa(u, carry):
        q_t = qt_ref[u]
        q_blk = 2 * u + q_half
        gate = jnp.dot(kmean, q_t, preferred_element_type=F32)
        rank = jnp.zeros((nb, PAIR), jnp.int32)
        for jp in range(nb):
            live = (q_blk > jp).astype(jnp.int32)
            tie = jnp.where(blk_id > jp, live, 0)
            g_jp = gate[jp:jp + 1, :]
            rank = rank + jnp.where(g_jp > gate, live, jnp.where(g_jp == gate, tie, 0))
        keep = jnp.where(blk_id < q_blk, rank, jnp.where(blk_id == q_blk, 0, TOPK)) < TOPK
        qa_ref[u, :HEAD_DIM] = q_t
        qa_ref[u, HEAD_DIM:HEAD_DIM + nb] = jnp.where(keep, 0.0, NEG).astype(BF16)
        return carry

    qa_ref[:, HEAD_DIM + nb:] = jnp.zeros((n_sb, LANES - nb, PAIR), BF16)
    lax.fori_loop(0, n_sb, build_qa, 0)

    def qk_logits(t, logit_ref):
        u = item_ref[0, t]
        jj = item_ref[1, t]
        r0 = pl.multiple_of(jj * PAIR, PAIR)
        raw = jnp.dot(k_ref[pl.ds(r0, PAIR), :], qa_ref[u], preferred_element_type=F32)
        d0 = 2 * (u - jj)
        tile = lambda d: bias_ref[jnp.clip(d, 0, n_tiles - 1)]
        top = raw[:BLOCK] + jnp.concatenate([tile(d0), tile(d0 + 1)], axis=1)
        bot = raw[BLOCK:] + jnp.concatenate([tile(d0 - 1), tile(d0)], axis=1)
        logit_ref[:BLOCK] = top
        logit_ref[BLOCK:] = bot
        return jnp.maximum(jnp.max(top, axis=0, keepdims=True), jnp.max(bot, axis=0, keepdims=True))

    def pv(t_prev, alpha_prev, acc, p_prev):
        return alpha_prev * acc + jnp.dot(vt_ref[item_ref[1, t_prev]], p_prev[...], preferred_element_type=F32)

    def step(t, c, logit_cur, logit_next, p_cur, p_prev):
        cmax, alpha_prev, m, acc = c
        cmax_next = qk_logits(jnp.minimum(t + 1, n_steps - 1), logit_next)
        acc = pv(jnp.maximum(t - 1, 0), alpha_prev, acc, p_prev)

        m_old = jnp.where(item_ref[1, t] == 0, NEG, m)
        m_new = jnp.maximum(m_old, cmax)
        alpha = jnp.exp2(m_old - m_new)
        p_cur[...] = jnp.exp2(logit_cur[...] - m_new).astype(BF16)
        return (cmax_next, alpha, m_new, acc), (t, acc)

    def store(u, acc):
        r0 = pl.multiple_of(u * PAIR, PAIR)
        o_ref[pl.ds(r0, PAIR), :] = (acc[:HEAD_DIM] * (1.0 / acc[HEAD_DIM:HEAD_DIM + 1])).T.astype(BF16)

    def finish(done):
        t, acc = done
        pl.when(item_ref[2, t] == 1)(lambda: store(item_ref[0, jnp.maximum(t - 1, 0)], acc))

    logit_refs = (logit0_ref, logit1_ref)
    p_refs = (p0_ref, p1_ref)

    def steps(t0, count, c):
        done = []
        for s in range(count):
            c, d = step(t0 + s, c, logit_refs[s % 2], logit_refs[1 - s % 2], p_refs[s % 2], p_refs[1 - s % 2])
            done.append(d)
        for d in done:
            finish(d)
        return c

    p1_ref[...] = jnp.zeros_like(p1_ref)
    c = (qk_logits(0, logit0_ref), jnp.zeros((1, PAIR), F32),
         jnp.full((1, PAIR), NEG, F32), jnp.zeros((V_ROWS, PAIR), F32))
    n_main = n_steps // ATTN_UNROLL
    c = lax.fori_loop(0, n_main, lambda tt, c: steps(ATTN_UNROLL * tt, ATTN_UNROLL, c), c)
    _, alpha_prev, _, acc = steps(n_main * ATTN_UNROLL, n_steps % ATTN_UNROLL, c)
    store(n_sb - 1, pv(n_steps - 1, alpha_prev, acc, p_refs[(n_steps - 1) % 2]))


def _attention(qt, k, vt, rel_bias_t, batch, seq, n_heads):
    nb = seq // BLOCK
    assert nb % 2 == 0 and nb <= LANES
    thr, ranges = _bias_plan(seq)
    items = _work_items(nb)
    n_steps = items.shape[1]
    m = batch * seq
    t_spec = lambda rows: pl.BlockSpec((None, seq // PAIR, rows, PAIR), lambda h, b: (h, b, 0, 0))
    return pl.pallas_call(
        functools.partial(_attn_kernel, nb=nb, thr=thr, ranges=ranges, n_steps=n_steps),
        grid=(n_heads, batch),
        in_specs=[pl.BlockSpec(memory_space=pltpu.SMEM),
                  pl.BlockSpec(memory_space=pltpu.SMEM),
                  t_spec(HEAD_DIM),
                  pl.BlockSpec((None, seq, 2 * HEAD_DIM), lambda h, b: (h, b, 0)),
                  t_spec(V_ROWS)],
        out_specs=pl.BlockSpec((seq, HEAD_DIM), lambda h, b: (b, h)),
        out_shape=jax.ShapeDtypeStruct((m, n_heads * HEAD_DIM), BF16),
        scratch_shapes=[pltpu.VMEM((len(ranges), BLOCK, BLOCK), F32),
                        pltpu.VMEM((nb, HEAD_DIM), F32),
                        pltpu.VMEM((nb // 2, 2 * LANES, PAIR), BF16),
                        pltpu.VMEM((PAIR, PAIR), F32),
                        pltpu.VMEM((PAIR, PAIR), F32),
                        pltpu.VMEM((PAIR, PAIR), BF16),
                        pltpu.VMEM((PAIR, PAIR), BF16)],
        compiler_params=_params("arbitrary", "arbitrary"),
        name="moba_attn",
    )(rel_bias_t, jnp.asarray(items), qt, k, vt)


def _outproj_kernel(attn_ref, conv_ref, wa_ref, wc_ref, w_ref, x_ref, ada_ref, o_ref, y_ref, *, k):
    j = pl.program_id(1)
    aw = attn_ref.shape[1]

    @pl.when(j == 0)
    def _():
        a = attn_ref[...].astype(F32)
        y_ref[:, :aw] = (a * lax.rsqrt(jnp.mean(a * a, axis=-1, keepdims=True) + EPS) * wa_ref[...]).astype(BF16)
        c = conv_ref[...].astype(F32)
        y_ref[:, aw:] = (c * lax.rsqrt(jnp.mean(c * c, axis=-1, keepdims=True) + EPS) * wc_ref[...]).astype(BF16)

    res = jnp.dot(y_ref[...], w_ref[...], preferred_element_type=F32)
    o_ref[...] = x_ref[...] + ada_ref[3 * k + 2:3 * k + 3, :] * res


def _outproj(attn, conv, wa, wc, w_out, x2d, ada, k, seq):
    m, aw = attn.shape
    cw = conv.shape[1]
    d = w_out.shape[1]
    tm = _pick(seq, 1024)
    tn = _pick(d, 512)
    return pl.pallas_call(
        functools.partial(_outproj_kernel, k=k),
        grid=(m // tm, d // tn),
        in_specs=[pl.BlockSpec((tm, aw), lambda i, j: (i, 0)),
                  pl.BlockSpec((tm, cw), lambda i, j: (i, 0)),
                  pl.BlockSpec((1, aw), lambda i, j: (0, 0)),
                  pl.BlockSpec((1, cw), lambda i, j: (0, 0)),
                  pl.BlockSpec((aw + cw, tn), lambda i, j: (0, j)),
                  pl.BlockSpec((tm, tn), lambda i, j: (i, j)),
                  pl.BlockSpec((None, N_ADA, tn), lambda i, j: (i * tm // seq, 0, j))],
        out_specs=pl.BlockSpec((tm, tn), lambda i, j: (i, j)),
        out_shape=jax.ShapeDtypeStruct((m, d), F32),
        scratch_shapes=[pltpu.VMEM((tm, aw + cw), BF16)],
        compiler_params=_params("parallel", "arbitrary"),
        name="out_proj",
    )(attn, conv, wa, wc, w_out, x2d, ada)


def _pad_to(a, axis, mult):
    pad = -a.shape[axis] % mult
    if not pad:
        return a
    widths = [(0, 0)] * a.ndim
    widths[axis] = (0, pad)
    return jnp.pad(a, widths)


def kernel(x, c, ada_w, ada_b, ffn1_norm, ffn1_w_gate, ffn1_w_up, ffn1_w_down, mix_norm, w_in, q_norm, k_norm,
           rel_bias, conv_w, attn_out_norm, conv_out_norm, w_out, ffn2_norm, ffn2_w_gate, ffn2_w_up, ffn2_w_down):
    batch, seq, d = x.shape
    depth = ada_w.shape[0]
    mix_w = w_out.shape[1]
    att_w = mix_w // 2
    conv_cw = mix_w - att_w
    n_heads = att_w // HEAD_DIM
    assert seq % BLOCK == 0 and att_w % HEAD_DIM == 0

    xs = x.reshape(batch * seq, d)
    c_pad = _pad_to(c, 0, SUBLANES)
    rel_bias_t = rel_bias.T
    for l in range(depth):
        ada = _ada(c_pad, ada_w[l], ada_b[l][None, :])[:batch].reshape(batch, N_ADA, d)

        h = _norm_mod(xs, ffn1_norm[l][None, :], ada, 0, seq)
        xs, (w_in_l,) = _ffn(h, ffn1_w_gate[l].astype(BF16), ffn1_w_up[l].astype(BF16), ffn1_w_down[l].astype(BF16),
                             xs, ada, 0, seq, to_cast=[(w_in[l], None)])

        h = _norm_mod(xs, mix_norm[l][None, :], ada, 1, seq)
        qt = _head_proj(h, w_in_l, 0, att_w, q_norm[l][None, :], seq, "q")
        kk = _head_proj(h, w_in_l, att_w, att_w, k_norm[l][None, :], seq, "k")
        vt = _head_proj(h, w_in_l, 2 * att_w, att_w, None, seq, "v")
        conv, (wg2, wu2, wd2, w_out_l) = _conv(h, w_in_l, 3 * att_w, conv_cw, conv_w[l], seq,
                                               [(ffn2_w_gate[l], FFN_TILE), (ffn2_w_up[l], FFN_TILE),
                                                (ffn2_w_down[l], None), (w_out[l], None)])
        attn = _attention(qt, kk, vt, rel_bias_t, batch, seq, n_heads)
        xs = _outproj(attn, conv, attn_out_norm[l][None, :], conv_out_norm[l][None, :], w_out_l, xs, ada, 1, seq)

        h = _norm_mod(xs, ffn2_norm[l][None, :], ada, 2, seq)
        xs, _ = _ffn(h, wg2, wu2, wd2, xs, ada, 2, seq)
    return xs.reshape(batch, seq, d)
```

```python
import functools
import math

import numpy as np
import jax
import jax.numpy as jnp
from jax import lax
from jax.experimental import pallas as pl
from jax.experimental.pallas import tpu as pltpu

HEAD_DIM = 128
BLOCK = 256
TOPK = 3
CONV_WIDTH = 3
N_BUCKETS = 32
MAX_DISTANCE = 2048
N_ADA = 9
EPS = 1e-6
NEG = -1e30
PAIR = 2 * BLOCK
ATTN_UNROLL = 2
V_ROWS = HEAD_DIM + 16
FFN_TILE = 512
LOG2E = 1.4426950408889634
QK_SCALE_LOG2 = HEAD_DIM ** -0.5 * LOG2E

LANES = 128
SUBLANES = 8
VMEM_LIMIT_BYTES = 60000 * 1024

F32 = jnp.float32
BF16 = jnp.bfloat16


def _params(*sem):
    return pltpu.CompilerParams(dimension_semantics=sem, vmem_limit_bytes=VMEM_LIMIT_BYTES)


def _pick(n, pref):
    if n <= pref:
        return n
    t = pref
    while n % t:
        t //= 2
    return t


def _cast_plan(to_cast, n_steps, step_of):
    in_specs, out_specs, out_shapes = [], [], []
    for a, tile in to_cast:
        rows = -(-a.shape[0] // n_steps)
        rows = -(-rows // (2 * SUBLANES)) * 2 * SUBLANES
        while a.shape[0] % rows:
            rows += 2 * SUBLANES
        block = lambda *g, last=a.shape[0] // rows - 1: jnp.minimum(step_of(*g), last)
        in_specs.append(pl.BlockSpec((rows, a.shape[1]), lambda *g, block=block: (block(*g), 0)))
        if tile is None:
            out_specs.append(in_specs[-1])
            out_shapes.append(jax.ShapeDtypeStruct(a.shape, BF16))
        else:
            n_tiles = -(-a.shape[1] // tile)
            out_specs.append(pl.BlockSpec((n_tiles, rows, tile), lambda *g, block=block: (0, block(*g), 0)))
            out_shapes.append(jax.ShapeDtypeStruct((n_tiles, a.shape[0], tile), BF16))
    return in_specs, out_specs, out_shapes


def _cast_streams(cast_in, cast_out):
    for src, dst in zip(cast_in, cast_out):
        if len(dst.shape) == 2:
            dst[...] = src[...].astype(BF16)
            continue
        n_tiles, rows, tile = dst.shape
        for t in range(n_tiles):
            width = min(tile, src.shape[1] - t * tile)
            dst[t, :, :width] = src[:, t * tile:t * tile + width].astype(BF16)
            if width < tile:
                dst[t, :, width:] = jnp.zeros((rows, tile - width), BF16)


def _ada_kernel(c_ref, w_ref, b_ref, o_ref):
    c = c_ref[...]
    cond = (c * jax.nn.sigmoid(c)).astype(BF16)
    o_ref[...] = jnp.dot(cond, w_ref[...].astype(BF16), preferred_element_type=F32) + b_ref[...]


def _ada(c_pad, ada_w, ada_b):
    rows, d = c_pad.shape
    n = ada_w.shape[1]
    tn = _pick(n, 512)
    return pl.pallas_call(
        _ada_kernel,
        grid=(n // tn,),
        in_specs=[pl.BlockSpec((rows, d), lambda j: (0, 0)),
                  pl.BlockSpec((d, tn), lambda j: (0, j)),
                  pl.BlockSpec((1, tn), lambda j: (0, j))],
        out_specs=pl.BlockSpec((rows, tn), lambda j: (0, j)),
        out_shape=jax.ShapeDtypeStruct((rows, n), F32),
        compiler_params=_params("parallel"),
        name="ada_proj",
    )(c_pad, ada_w, ada_b)


def _norm_mod_kernel(x_ref, w_ref, ada_ref, o_ref, *, k):
    x = x_ref[...]
    ms = jnp.mean(x * x, axis=-1, keepdims=True)
    y = x * lax.rsqrt(ms + EPS) * w_ref[...]
    shift = ada_ref[3 * k:3 * k + 1, :]
    scale = ada_ref[3 * k + 1:3 * k + 2, :]
    o_ref[...] = (y * (1.0 + scale) + shift).astype(BF16)


def _norm_mod(x2d, w, ada, k, seq):
    m, d = x2d.shape
    tr = _pick(seq, 512)
    return pl.pallas_call(
        functools.partial(_norm_mod_kernel, k=k),
        grid=(m // tr,),
        in_specs=[pl.BlockSpec((tr, d), lambda i: (i, 0)),
                  pl.BlockSpec((1, d), lambda i: (0, 0)),
                  pl.BlockSpec((None, N_ADA, d), lambda i: (i * tr // seq, 0, 0))],
        out_specs=pl.BlockSpec((tr, d), lambda i: (i, 0)),
        out_shape=jax.ShapeDtypeStruct((m, d), BF16),
        compiler_params=_params("parallel"),
        name=f"norm_mod{k}",
    )(x2d, w, ada)


def _ffn_kernel(h_ref, wg_ref, wu_ref, wd_ref, x_ref, ada_ref, *refs, k, n_xchunks, xc, nchunk, tail, n_cast):
    cast_in, o_ref, cast_out = refs[:n_cast], refs[n_cast], refs[n_cast + 1:]
    _cast_streams(cast_in, cast_out)
    f = pl.program_id(1)
    nf = pl.num_programs(1)
    d = o_ref.shape[1]
    tf = wg_ref.shape[1]

    @pl.when(f == 0)
    def _():
        o_ref[...] = jnp.zeros_like(o_ref)

    for c in range(n_xchunks):
        @pl.when(f == c)
        def _(c=c):
            o_ref[:, c * xc:(c + 1) * xc] += x_ref[...]

    def accumulate(width):
        h = h_ref[...]
        g = jnp.dot(h, wg_ref[:, :width], preferred_element_type=F32)
        u = jnp.dot(h, wu_ref[:, :width], preferred_element_type=F32)
        a = (g * jax.nn.sigmoid(g) * u).astype(BF16)
        half_gate = 0.5 * ada_ref[3 * k + 2:3 * k + 3, :]
        for n0 in range(0, d, nchunk):
            part = jnp.dot(a, wd_ref[:width, n0:n0 + nchunk], preferred_element_type=F32)
            o_ref[:, n0:n0 + nchunk] += half_gate[:, n0:n0 + nchunk] * part

    if tail == tf:
        accumulate(tf)
    else:
        pl.when(f < nf - 1)(lambda: accumulate(tf))
        pl.when(f == nf - 1)(lambda: accumulate(tail))


def _ffn(h, wg, wu, wd, x2d, ada, k, seq, to_cast=(), tm_pref=512):
    m, d = h.shape
    d_ff = wd.shape[0]
    tm = _pick(seq, tm_pref)
    tf = min(FFN_TILE, d_ff)
    nf = pl.cdiv(d_ff, tf)
    tail = d_ff - (nf - 1) * tf
    assert tail % LANES == 0, (d_ff, tf)
    n_xchunks = 1
    while d % n_xchunks or (d // n_xchunks) % LANES or d // n_xchunks > 512:
        n_xchunks += 1
    assert n_xchunks <= nf, (n_xchunks, nf)
    xc = d // n_xchunks
    nchunk = _pick(d, 512)
    cast_in, cast_out, cast_shapes = _cast_plan(to_cast, (m // tm) * nf, lambda i, f: i * nf + f)
    if wg.ndim == 3:
        assert wg.shape == wu.shape == (nf, d, tf), (wg.shape, wu.shape)
        w_spec = pl.BlockSpec((None, d, tf), lambda i, f: (f, 0, 0))
    else:
        w_spec = pl.BlockSpec((d, tf), lambda i, f: (0, f))
    outs = pl.pallas_call(
        functools.partial(_ffn_kernel, k=k, n_xchunks=n_xchunks, xc=xc, nchunk=nchunk, tail=tail,
                          n_cast=len(to_cast)),
        grid=(m // tm, nf),
        in_specs=[pl.BlockSpec((tm, d), lambda i, f: (i, 0)),
                  w_spec,
                  w_spec,
                  pl.BlockSpec((tf, d), lambda i, f: (f, 0)),
                  pl.BlockSpec((tm, xc), lambda i, f: (i, jnp.minimum(f, n_xchunks - 1))),
                  pl.BlockSpec((None, N_ADA, d), lambda i, f: (i * tm // seq, 0, 0))] + cast_in,
        out_specs=[pl.BlockSpec((tm, d), lambda i, f: (i, 0))] + cast_out,
        out_shape=[jax.ShapeDtypeStruct((m, d), F32)] + cast_shapes,
        compiler_params=_params("arbitrary", "arbitrary"),
        name=f"ffn{k}",
    )(h, wg, wu, wd, x2d, ada, *[a for a, _ in to_cast])
    return outs[0], outs[1:]


def _head_norm(blk, w):
    ms = jnp.mean(blk * blk, axis=-1, keepdims=True)
    return blk * lax.rsqrt(ms + EPS) * w


def _k_kernel(h_ref, w_ref, nw_ref, o_ref, *, heads_per_tile, seq):
    tm = h_ref.shape[0]
    res = jnp.dot(h_ref[...], w_ref[...], preferred_element_type=F32)
    pos = (pl.program_id(0) * tm) % seq + lax.broadcasted_iota(jnp.int32, (tm, LANES), 0)
    lane = lax.broadcasted_iota(jnp.int32, (tm, LANES), 1)
    onehot = jnp.where(lane * BLOCK == pos - pos % BLOCK, 1.0, 0.0).astype(BF16)
    for hh in range(heads_per_tile):
        o_ref[hh, :, :HEAD_DIM] = _head_norm(res[:, hh * HEAD_DIM:(hh + 1) * HEAD_DIM], nw_ref[...]).astype(BF16)
        o_ref[hh, :, HEAD_DIM:] = onehot


def _store_transposed(blk, o_ref, hh):
    blk_t = blk.T.astype(BF16)
    for pp in range(o_ref.shape[1]):
        o_ref[hh, pp, :HEAD_DIM] = blk_t[:, pp * PAIR:(pp + 1) * PAIR]


def _qt_kernel(h_ref, w_ref, nw_ref, o_ref, *, heads_per_tile):
    res = jnp.dot(h_ref[...], w_ref[...], preferred_element_type=F32)
    for hh in range(heads_per_tile):
        blk = _head_norm(res[:, hh * HEAD_DIM:(hh + 1) * HEAD_DIM], nw_ref[...])
        _store_transposed(blk * QK_SCALE_LOG2, o_ref, hh)


def _vt_kernel(h_ref, w_ref, o_ref, *, heads_per_tile):
    res = jnp.dot(h_ref[...], w_ref[...], preferred_element_type=F32)
    for hh in range(heads_per_tile):
        _store_transposed(res[:, hh * HEAD_DIM:(hh + 1) * HEAD_DIM], o_ref, hh)
        o_ref[hh, :, HEAD_DIM:] = jnp.ones((o_ref.shape[1], V_ROWS - HEAD_DIM, PAIR), BF16)


def _head_proj(h, w, col0, n, norm_w, seq, kind):
    m, d = h.shape
    tm = _pick(seq, 1024)
    tn = _pick(n, 1024)
    assert col0 % tn == 0
    hpt = tn // HEAD_DIM
    in_specs = [pl.BlockSpec((tm, d), lambda i, j: (i, 0)),
                pl.BlockSpec((d, tn), lambda i, j: (0, col0 // tn + j))]
    args = [h, w]
    if kind != "v":
        in_specs.append(pl.BlockSpec((1, HEAD_DIM), lambda i, j: (0, 0)))
        args.append(norm_w)
    if kind == "k":
        body = functools.partial(_k_kernel, heads_per_tile=hpt, seq=seq)
        out_spec = pl.BlockSpec((hpt, tm, 2 * HEAD_DIM), lambda i, j: (j, i, 0))
        out_shape = (n // HEAD_DIM, m, 2 * HEAD_DIM)
    else:
        body = functools.partial(_qt_kernel if kind == "q" else _vt_kernel, heads_per_tile=hpt)
        rows = HEAD_DIM if kind == "q" else V_ROWS
        out_spec = pl.BlockSpec((hpt, tm // PAIR, rows, PAIR), lambda i, j: (j, i, 0, 0))
        out_shape = (n // HEAD_DIM, m // PAIR, rows, PAIR)
    return pl.pallas_call(
        body,
        grid=(m // tm, n // tn),
        in_specs=in_specs,
        out_specs=out_spec,
        out_shape=jax.ShapeDtypeStruct(out_shape, BF16),
        compiler_params=_params("parallel", "arbitrary"),
        name=f"{kind}_proj",
    )(*args)


def _conv_kernel(h_ref, wc_ref, wb_ref, wh_ref, cw_ref, *refs, tiles_per_batch, n_cast):
    cast_in, o_ref, cast_out, ubuf_ref = refs[:n_cast], refs[n_cast], refs[n_cast + 1:2 * n_cast + 1], refs[-1]
    _cast_streams(cast_in, cast_out)
    i = pl.program_id(1)
    tm = h_ref.shape[0]

    @pl.when(i % tiles_per_batch == 0)
    def _():
        ubuf_ref[0:SUBLANES, :] = jnp.zeros((SUBLANES, ubuf_ref.shape[1]), F32)

    h = h_ref[...]
    cg = jnp.dot(h, wc_ref[...], preferred_element_type=F32)
    hc = jnp.dot(h, wh_ref[...], preferred_element_type=F32)
    ubuf_ref[SUBLANES:SUBLANES + tm, :] = cg * hc
    u0 = ubuf_ref[SUBLANES:SUBLANES + tm, :]
    u1 = ubuf_ref[SUBLANES - 1:SUBLANES - 1 + tm, :]
    u2 = ubuf_ref[SUBLANES - 2:SUBLANES - 2 + tm, :]
    y = cw_ref[0:1, :] * u2 + cw_ref[1:2, :] * u1 + cw_ref[2:3, :] * u0
    bg = jnp.dot(h, wb_ref[...], preferred_element_type=F32)
    o_ref[...] = (bg * y).astype(BF16)
    ubuf_ref[0:SUBLANES, :] = ubuf_ref[tm:tm + SUBLANES, :]


def _conv(h, w, col0, cw, conv_w, seq, to_cast):
    m, d = h.shape
    tm = _pick(seq, 1024)
    tc = _pick(cw, 256)
    assert col0 % tc == 0
    n_i = m // tm
    wspec = lambda group: pl.BlockSpec((d, tc), lambda c, i: (0, (col0 + group * cw) // tc + c))
    cast_in, cast_out, cast_shapes = _cast_plan(to_cast, (cw // tc) * n_i, lambda c, i: c * n_i + i)
    outs = pl.pallas_call(
        functools.partial(_conv_kernel, tiles_per_batch=seq // tm, n_cast=len(to_cast)),
        grid=(cw // tc, n_i),
        in_specs=[pl.BlockSpec((tm, d), lambda c, i: (i, 0)), wspec(0), wspec(1), wspec(2),
                  pl.BlockSpec((CONV_WIDTH, tc), lambda c, i: (0, c))] + cast_in,
        out_specs=[pl.BlockSpec((tm, tc), lambda c, i: (i, c))] + cast_out,
        out_shape=[jax.ShapeDtypeStruct((m, cw), BF16)] + cast_shapes,
        scratch_shapes=[pltpu.VMEM((tm + SUBLANES, tc), F32)],
        compiler_params=_params("arbitrary", "arbitrary"),
        name="conv_proj",
    )(h, w, w, w, conv_w, *[a for a, _ in to_cast])
    return outs[0], outs[1:]


def _t5_bucket_np(dist):
    max_exact = N_BUCKETS // 2
    d = np.maximum(dist, 1).astype(np.float32)
    large = max_exact + (np.log(d / np.float32(max_exact)) / np.float32(math.log(MAX_DISTANCE / max_exact))
                         * np.float32(N_BUCKETS - max_exact)).astype(np.int32)
    large = np.minimum(large, N_BUCKETS - 1)
    return np.where(dist < max_exact, dist, large)


def _bias_plan(seq):
    buckets = _t5_bucket_np(np.arange(seq, dtype=np.int64))
    assert (np.diff(buckets) >= 0).all()
    thr = [int(np.argmax(buckets >= b)) if (buckets >= b).any() else seq for b in range(N_BUCKETS)]
    nb = seq // BLOCK
    n_tiles = nb
    for delta in range(1, nb):
        if buckets[delta * BLOCK - (BLOCK - 1)] == buckets[-1]:
            n_tiles = delta + 1
            break
    ranges = []
    for delta in range(n_tiles):
        lo = max(0, delta * BLOCK - (BLOCK - 1))
        hi = min(seq - 1, delta * BLOCK + (BLOCK - 1))
        ranges.append((int(buckets[lo]), int(buckets[hi])))
    return thr, ranges


def _work_items(nb):
    items = [(u, jj) for u in range(nb // 2) for jj in range(u + 1)]
    tab = np.zeros((3, len(items)), np.int32)
    for t, (u, jj) in enumerate(items):
        tab[0, t], tab[1, t] = u, jj
        tab[2, t] = 1 if t >= 1 and items[t - 1][1] == items[t - 1][0] else 0
    return tab


def _attn_kernel(tbl_ref, item_ref, qt_ref, k_ref, vt_ref, o_ref, bias_ref, kmean_ref, qa_ref,
                 logit0_ref, logit1_ref, p0_ref, p1_ref, *, nb, thr, ranges, n_steps):
    hd = pl.program_id(0)
    b = pl.program_id(1)
    n_tiles = len(ranges)
    n_sb = nb // 2

    @pl.when(b == 0)
    def _build_bias():
        rel = (lax.broadcasted_iota(jnp.int32, (BLOCK, BLOCK), 1)
               - lax.broadcasted_iota(jnp.int32, (BLOCK, BLOCK), 0))
        for delta, (b_lo, b_hi) in enumerate(ranges):
            dist = rel + delta * BLOCK
            val = jnp.full((BLOCK, BLOCK), tbl_ref[hd, b_lo], F32)
            for bk in range(b_lo + 1, b_hi + 1):
                val = jnp.where(dist >= thr[bk], tbl_ref[hd, bk], val)
            val = val * LOG2E
            if delta == 0:
                val = jnp.where(rel >= 0, val, NEG)
            bias_ref[delta] = val

    for j in range(nb):
        kj = k_ref[j * BLOCK:(j + 1) * BLOCK, :HEAD_DIM].astype(F32)
        kmean_ref[j:j + 1, :] = jnp.sum(kj, axis=0, keepdims=True) * (1.0 / BLOCK)
    kmean = kmean_ref[...].astype(BF16)

    blk_id = lax.broadcasted_iota(jnp.int32, (nb, PAIR), 0)
    q_half = (lax.broadcasted_iota(jnp.int32, (1, PAIR), 1) >= BLOCK).astype(jnp.int32)

    def build_qa(u, carry):
        q_t = qt_ref[u]
        q_blk = 2 * u + q_half
        gate = jnp.dot(kmean, q_t, preferred_element_type=F32)
        rank = jnp.zeros((nb, PAIR), jnp.int32)
        for jp in range(nb):
            live = (q_blk > jp).astype(jnp.int32)
            tie = jnp.where(blk_id > jp, live, 0)
            g_jp = gate[jp:jp + 1, :]
            rank = rank + jnp.where(g_jp > gate, live, jnp.where(g_jp == gate, tie, 0))
        keep = jnp.where(blk_id < q_blk, rank, jnp.where(blk_id == q_blk, 0, TOPK)) < TOPK
        qa_ref[u, :HEAD_DIM] = q_t
        qa_ref[u, HEAD_DIM:HEAD_DIM + nb] = jnp.where(keep, 0.0, NEG).astype(BF16)
        return carry

    qa_ref[:, HEAD_DIM + nb:] = jnp.zeros((n_sb, LANES - nb, PAIR), BF16)
    for u in range(n_sb):
        build_qa(u, 0)

    def qk_logits(t, logit_ref):
        u = item_ref[0, t]
        jj = item_ref[1, t]
        r0 = pl.multiple_of(jj * PAIR, PAIR)
        raw = jnp.dot(k_ref[pl.ds(r0, PAIR), :], qa_ref[u], preferred_element_type=F32)
        d0 = 2 * (u - jj)
        tile = lambda d: bias_ref[jnp.clip(d, 0, n_tiles - 1)]
        top = raw[:BLOCK] + jnp.concatenate([tile(d0), tile(d0 + 1)], axis=1)
        bot = raw[BLOCK:] + jnp.concatenate([tile(d0 - 1), tile(d0)], axis=1)
        logit_ref[:BLOCK] = top
        logit_ref[BLOCK:] = bot
        return jnp.maximum(jnp.max(top, axis=0, keepdims=True), jnp.max(bot, axis=0, keepdims=True))

    def pv(t_prev, alpha_prev, acc, p_prev):
        return alpha_prev * acc + jnp.dot(vt_ref[item_ref[1, t_prev]], p_prev[...], preferred_element_type=F32)

    def step(t, c, logit_cur, logit_next, p_cur, p_prev):
        cmax, alpha_prev, m, acc = c
        cmax_next = qk_logits(jnp.minimum(t + 1, n_steps - 1), logit_next)
        acc = pv(jnp.maximum(t - 1, 0), alpha_prev, acc, p_prev)

        m_old = jnp.where(item_ref[1, t] == 0, NEG, m)
        m_new = jnp.maximum(m_old, cmax)
        alpha = jnp.exp2(m_old - m_new)
        p_cur[...] = jnp.exp2(logit_cur[...] - m_new).astype(BF16)
        return (cmax_next, alpha, m_new, acc), (t, acc)

    def store(u, acc):
        r0 = pl.multiple_of(u * PAIR, PAIR)
        o_ref[pl.ds(r0, PAIR), :] = (acc[:HEAD_DIM] * (1.0 / acc[HEAD_DIM:HEAD_DIM + 1])).T.astype(BF16)

    def finish(done):
        t, acc = done
        pl.when(item_ref[2, t] == 1)(lambda: store(item_ref[0, jnp.maximum(t - 1, 0)], acc))

    logit_refs = (logit0_ref, logit1_ref)
    p_refs = (p0_ref, p1_ref)

    def steps(t0, count, c):
        done = []
        for s in range(count):
            c, d = step(t0 + s, c, logit_refs[s % 2], logit_refs[1 - s % 2], p_refs[s % 2], p_refs[1 - s % 2])
            done.append(d)
        for d in done:
            finish(d)
        return c

    p1_ref[...] = jnp.zeros_like(p1_ref)
    c = (qk_logits(0, logit0_ref), jnp.zeros((1, PAIR), F32),
         jnp.full((1, PAIR), NEG, F32), jnp.zeros((V_ROWS, PAIR), F32))
    n_main = n_steps // ATTN_UNROLL
    c = lax.fori_loop(0, n_main, lambda tt, c: steps(ATTN_UNROLL * tt, ATTN_UNROLL, c), c)
    _, alpha_prev, _, acc = steps(n_main * ATTN_UNROLL, n_steps % ATTN_UNROLL, c)
    store(n_sb - 1, pv(n_steps - 1, alpha_prev, acc, p_refs[(n_steps - 1) % 2]))


def _attention(qt, k, vt, rel_bias_t, batch, seq, n_heads):
    nb = seq // BLOCK
    assert nb % 2 == 0 and nb <= LANES
    thr, ranges = _bias_plan(seq)
    items = _work_items(nb)
    n_steps = items.shape[1]
    m = batch * seq
    t_spec = lambda rows: pl.BlockSpec((None, seq // PAIR, rows, PAIR), lambda h, b: (h, b, 0, 0))
    return pl.pallas_call(
        functools.partial(_attn_kernel, nb=nb, thr=thr, ranges=ranges, n_steps=n_steps),
        grid=(n_heads, batch),
        in_specs=[pl.BlockSpec(memory_space=pltpu.SMEM),
                  pl.BlockSpec(memory_space=pltpu.SMEM),
                  t_spec(HEAD_DIM),
                  pl.BlockSpec((None, seq, 2 * HEAD_DIM), lambda h, b: (h, b, 0)),
                  t_spec(V_ROWS)],
        out_specs=pl.BlockSpec((seq, HEAD_DIM), lambda h, b: (b, h)),
        out_shape=jax.ShapeDtypeStruct((m, n_heads * HEAD_DIM), BF16),
        scratch_shapes=[pltpu.VMEM((len(ranges), BLOCK, BLOCK), F32),
                        pltpu.VMEM((nb, HEAD_DIM), F32),
                        pltpu.VMEM((nb // 2, 2 * LANES, PAIR), BF16),
                        pltpu.VMEM((PAIR, PAIR), F32),
                        pltpu.VMEM((PAIR, PAIR), F32),
                        pltpu.VMEM((PAIR, PAIR), BF16),
                        pltpu.VMEM((PAIR, PAIR), BF16)],
        compiler_params=_params("arbitrary", "arbitrary"),
        name="moba_attn",
    )(rel_bias_t, jnp.asarray(items), qt, k, vt)


def _outproj_kernel(attn_ref, conv_ref, wa_ref, wc_ref, w_ref, x_ref, ada_ref, o_ref, y_ref, *, k):
    j = pl.program_id(1)
    aw = attn_ref.shape[1]

    @pl.when(j == 0)
    def _():
        a = attn_ref[...].astype(F32)
        y_ref[:, :aw] = (a * lax.rsqrt(jnp.mean(a * a, axis=-1, keepdims=True) + EPS) * wa_ref[...]).astype(BF16)
        c = conv_ref[...].astype(F32)
        y_ref[:, aw:] = (c * lax.rsqrt(jnp.mean(c * c, axis=-1, keepdims=True) + EPS) * wc_ref[...]).astype(BF16)

    res = jnp.dot(y_ref[...], w_ref[...], preferred_element_type=F32)
    o_ref[...] = x_ref[...] + ada_ref[3 * k + 2:3 * k + 3, :] * res


def _outproj(attn, conv, wa, wc, w_out, x2d, ada, k, seq):
    m, aw = attn.shape
    cw = conv.shape[1]
    d = w_out.shape[1]
    tm = _pick(seq, 1024)
    tn = _pick(d, 512)
    return pl.pallas_call(
        functools.partial(_outproj_kernel, k=k),
        grid=(m // tm, d // tn),
        in_specs=[pl.BlockSpec((tm, aw), lambda i, j: (i, 0)),
                  pl.BlockSpec((tm, cw), lambda i, j: (i, 0)),
                  pl.BlockSpec((1, aw), lambda i, j: (0, 0)),
                  pl.BlockSpec((1, cw), lambda i, j: (0, 0)),
                  pl.BlockSpec((aw + cw, tn), lambda i, j: (0, j)),
                  pl.BlockSpec((tm, tn), lambda i, j: (i, j)),
                  pl.BlockSpec((None, N_ADA, tn), lambda i, j: (i * tm // seq, 0, j))],
        out_specs=pl.BlockSpec((tm, tn), lambda i, j: (i, j)),
        out_shape=jax.ShapeDtypeStruct((m, d), F32),
        scratch_shapes=[pltpu.VMEM((tm, aw + cw), BF16)],
        compiler_params=_params("parallel", "arbitrary"),
        name="out_proj",
    )(attn, conv, wa, wc, w_out, x2d, ada)


def _pad_to(a, axis, mult):
    pad = -a.shape[axis] % mult
    if not pad:
        return a
    widths = [(0, 0)] * a.ndim
    widths[axis] = (0, pad)
    return jnp.pad(a, widths)


def kernel(x, c, ada_w, ada_b, ffn1_norm, ffn1_w_gate, ffn1_w_up, ffn1_w_down, mix_norm, w_in, q_norm, k_norm,
           rel_bias, conv_w, attn_out_norm, conv_out_norm, w_out, ffn2_norm, ffn2_w_gate, ffn2_w_up, ffn2_w_down):
    batch, seq, d = x.shape
    depth = ada_w.shape[0]
    mix_w = w_out.shape[1]
    att_w = mix_w // 2
    conv_cw = mix_w - att_w
    n_heads = att_w // HEAD_DIM
    assert seq % BLOCK == 0 and att_w % HEAD_DIM == 0

    xs = x.reshape(batch * seq, d)
    c_pad = _pad_to(c, 0, SUBLANES)
    rel_bias_t = rel_bias.T
    for l in range(depth):
        ada = _ada(c_pad, ada_w[l], ada_b[l][None, :])[:batch].reshape(batch, N_ADA, d)

        h = _norm_mod(xs, ffn1_norm[l][None, :], ada, 0, seq)
        xs, (w_in_l,) = _ffn(h, ffn1_w_gate[l].astype(BF16), ffn1_w_up[l].astype(BF16), ffn1_w_down[l].astype(BF16),
                             xs, ada, 0, seq, to_cast=[(w_in[l], None)])

        h = _norm_mod(xs, mix_norm[l][None, :], ada, 1, seq)
        qt = _head_proj(h, w_in_l, 0, att_w, q_norm[l][None, :], seq, "q")
        kk = _head_proj(h, w_in_l, att_w, att_w, k_norm[l][None, :], seq, "k")
        vt = _head_proj(h, w_in_l, 2 * att_w, att_w, None, seq, "v")
        conv, (wg2, wu2, wd2, w_out_l) = _conv(h, w_in_l, 3 * att_w, conv_cw, conv_w[l], seq,
                                               [(ffn2_w_gate[l], FFN_TILE), (ffn2_w_up[l], FFN_TILE),
                                                (ffn2_w_down[l], None), (w_out[l], None)])
        attn = _attention(qt, kk, vt, rel_bias_t, batch, seq, n_heads)
        xs = _outproj(attn, conv, attn_out_norm[l][None, :], conv_out_norm[l][None, :], w_out_l, xs, ada, 1, seq)

        h = _norm_mod(xs, ffn2_norm[l][None, :], ada, 2, seq)
        xs, _ = _ffn(h, wg2, wu2, wd2, xs, ada, 2, seq)
    return xs.reshape(batch, seq, d)
```

```python
import functools
import math

import numpy as np
import jax
import jax.numpy as jnp
from jax import lax
from jax.experimental import pallas as pl
from jax.experimental.pallas import tpu as pltpu

HEAD_DIM = 128
BLOCK = 256
TOPK = 3
CONV_WIDTH = 3
N_BUCKETS = 32
MAX_DISTANCE = 2048
N_ADA = 9
EPS = 1e-6
NEG = -1e30
PAIR = 2 * BLOCK
ATTN_UNROLL = 4
V_ROWS = HEAD_DIM + 16
FFN_TILE = 512
LOG2E = 1.4426950408889634
QK_SCALE_LOG2 = HEAD_DIM ** -0.5 * LOG2E

LANES = 128
SUBLANES = 8
VMEM_LIMIT_BYTES = 60000 * 1024

F32 = jnp.float32
BF16 = jnp.bfloat16


def _params(*sem):
    return pltpu.CompilerParams(dimension_semantics=sem, vmem_limit_bytes=VMEM_LIMIT_BYTES)


def _pick(n, pref):
    if n <= pref:
        return n
    t = pref
    while n % t:
        t //= 2
    return t


def _cast_plan(to_cast, n_steps, step_of):
    in_specs, out_specs, out_shapes = [], [], []
    for a, tile in to_cast:
        rows = -(-a.shape[0] // n_steps)
        rows = -(-rows // (2 * SUBLANES)) * 2 * SUBLANES
        while a.shape[0] % rows:
            rows += 2 * SUBLANES
        block = lambda *g, last=a.shape[0] // rows - 1: jnp.minimum(step_of(*g), last)
        in_specs.append(pl.BlockSpec((rows, a.shape[1]), lambda *g, block=block: (block(*g), 0)))
        if tile is None:
            out_specs.append(in_specs[-1])
            out_shapes.append(jax.ShapeDtypeStruct(a.shape, BF16))
        else:
            n_tiles = -(-a.shape[1] // tile)
            out_specs.append(pl.BlockSpec((n_tiles, rows, tile), lambda *g, block=block: (0, block(*g), 0)))
            out_shapes.append(jax.ShapeDtypeStruct((n_tiles, a.shape[0], tile), BF16))
    return in_specs, out_specs, out_shapes


def _cast_streams(cast_in, cast_out):
    for src, dst in zip(cast_in, cast_out):
        if len(dst.shape) == 2:
            dst[...] = src[...].astype(BF16)
            continue
        n_tiles, rows, tile = dst.shape
        for t in range(n_tiles):
            width = min(tile, src.shape[1] - t * tile)
            dst[t, :, :width] = src[:, t * tile:t * tile + width].astype(BF16)
            if width < tile:
                dst[t, :, width:] = jnp.zeros((rows, tile - width), BF16)


def _ada_kernel(c_ref, w_ref, b_ref, o_ref):
    c = c_ref[...]
    cond = (c * jax.nn.sigmoid(c)).astype(BF16)
    o_ref[...] = jnp.dot(cond, w_ref[...].astype(BF16), preferred_element_type=F32) + b_ref[...]


def _ada(c_pad, ada_w, ada_b):
    rows, d = c_pad.shape
    n = ada_w.shape[1]
    tn = _pick(n, 512)
    return pl.pallas_call(
        _ada_kernel,
        grid=(n // tn,),
        in_specs=[pl.BlockSpec((rows, d), lambda j: (0, 0)),
                  pl.BlockSpec((d, tn), lambda j: (0, j)),
                  pl.BlockSpec((1, tn), lambda j: (0, j))],
        out_specs=pl.BlockSpec((rows, tn), lambda j: (0, j)),
        out_shape=jax.ShapeDtypeStruct((rows, n), F32),
        compiler_params=_params("parallel"),
        name="ada_proj",
    )(c_pad, ada_w, ada_b)


def _norm_mod_kernel(x_ref, w_ref, ada_ref, o_ref, *, k):
    x = x_ref[...]
    ms = jnp.mean(x * x, axis=-1, keepdims=True)
    y = x * lax.rsqrt(ms + EPS) * w_ref[...]
    shift = ada_ref[3 * k:3 * k + 1, :]
    scale = ada_ref[3 * k + 1:3 * k + 2, :]
    o_ref[...] = (y * (1.0 + scale) + shift).astype(BF16)


def _norm_mod(x2d, w, ada, k, seq):
    m, d = x2d.shape
    tr = _pick(seq, 512)
    return pl.pallas_call(
        functools.partial(_norm_mod_kernel, k=k),
        grid=(m // tr,),
        in_specs=[pl.BlockSpec((tr, d), lambda i: (i, 0)),
                  pl.BlockSpec((1, d), lambda i: (0, 0)),
                  pl.BlockSpec((None, N_ADA, d), lambda i: (i * tr // seq, 0, 0))],
        out_specs=pl.BlockSpec((tr, d), lambda i: (i, 0)),
        out_shape=jax.ShapeDtypeStruct((m, d), BF16),
        compiler_params=_params("parallel"),
        name=f"norm_mod{k}",
    )(x2d, w, ada)


def _ffn_kernel(h_ref, wg_ref, wu_ref, wd_ref, x_ref, ada_ref, *refs, k, n_xchunks, xc, nchunk, tail, n_cast):
    cast_in, o_ref, cast_out = refs[:n_cast], refs[n_cast], refs[n_cast + 1:]
    _cast_streams(cast_in, cast_out)
    f = pl.program_id(1)
    nf = pl.num_programs(1)
    d = o_ref.shape[1]
    tf = wg_ref.shape[1]

    @pl.when(f == 0)
    def _():
        o_ref[...] = jnp.zeros_like(o_ref)

    for c in range(n_xchunks):
        @pl.when(f == c)
        def _(c=c):
            o_ref[:, c * xc:(c + 1) * xc] += x_ref[...]

    def accumulate(width):
        h = h_ref[...]
        g = jnp.dot(h, wg_ref[:, :width], preferred_element_type=F32)
        u = jnp.dot(h, wu_ref[:, :width], preferred_element_type=F32)
        a = (g * jax.nn.sigmoid(g) * u).astype(BF16)
        half_gate = 0.5 * ada_ref[3 * k + 2:3 * k + 3, :]
        for n0 in range(0, d, nchunk):
            part = jnp.dot(a, wd_ref[:width, n0:n0 + nchunk], preferred_element_type=F32)
            o_ref[:, n0:n0 + nchunk] += half_gate[:, n0:n0 + nchunk] * part

    if tail == tf:
        accumulate(tf)
    else:
        pl.when(f < nf - 1)(lambda: accumulate(tf))
        pl.when(f == nf - 1)(lambda: accumulate(tail))


def _ffn(h, wg, wu, wd, x2d, ada, k, seq, to_cast=(), tm_pref=512):
    m, d = h.shape
    d_ff = wd.shape[0]
    tm = _pick(seq, tm_pref)
    tf = min(FFN_TILE, d_ff)
    nf = pl.cdiv(d_ff, tf)
    tail = d_ff - (nf - 1) * tf
    assert tail % LANES == 0, (d_ff, tf)
    n_xchunks = 1
    while d % n_xchunks or (d // n_xchunks) % LANES or d // n_xchunks > 512:
        n_xchunks += 1
    assert n_xchunks <= nf, (n_xchunks, nf)
    xc = d // n_xchunks
    nchunk = _pick(d, 512)
    cast_in, cast_out, cast_shapes = _cast_plan(to_cast, (m // tm) * nf, lambda i, f: i * nf + f)
    if wg.ndim == 3:
        assert wg.shape == wu.shape == (nf, d, tf), (wg.shape, wu.shape)
        w_spec = pl.BlockSpec((None, d, tf), lambda i, f: (f, 0, 0))
    else:
        w_spec = pl.BlockSpec((d, tf), lambda i, f: (0, f))
    outs = pl.pallas_call(
        functools.partial(_ffn_kernel, k=k, n_xchunks=n_xchunks, xc=xc, nchunk=nchunk, tail=tail,
                          n_cast=len(to_cast)),
        grid=(m // tm, nf),
        in_specs=[pl.BlockSpec((tm, d), lambda i, f: (i, 0)),
                  w_spec,
                  w_spec,
                  pl.BlockSpec((tf, d), lambda i, f: (f, 0)),
                  pl.BlockSpec((tm, xc), lambda i, f: (i, jnp.minimum(f, n_xchunks - 1))),
                  pl.BlockSpec((None, N_ADA, d), lambda i, f: (i * tm // seq, 0, 0))] + cast_in,
        out_specs=[pl.BlockSpec((tm, d), lambda i, f: (i, 0))] + cast_out,
        out_shape=[jax.ShapeDtypeStruct((m, d), F32)] + cast_shapes,
        compiler_params=_params("arbitrary", "arbitrary"),
        name=f"ffn{k}",
    )(h, wg, wu, wd, x2d, ada, *[a for a, _ in to_cast])
    return outs[0], outs[1:]


def _head_norm(blk, w):
    ms = jnp.mean(blk * blk, axis=-1, keepdims=True)
    return blk * lax.rsqrt(ms + EPS) * w


def _k_kernel(h_ref, w_ref, nw_ref, o_ref, *, heads_per_tile, seq):
    tm = h_ref.shape[0]
    res = jnp.dot(h_ref[...], w_ref[...], preferred_element_type=F32)
    pos = (pl.program_id(0) * tm) % seq + lax.broadcasted_iota(jnp.int32, (tm, LANES), 0)
    lane = lax.broadcasted_iota(jnp.int32, (tm, LANES), 1)
    onehot = jnp.where(lane * BLOCK == pos - pos % BLOCK, 1.0, 0.0).astype(BF16)
    for hh in range(heads_per_tile):
        o_ref[hh, :, :HEAD_DIM] = _head_norm(res[:, hh * HEAD_DIM:(hh + 1) * HEAD_DIM], nw_ref[...]).astype(BF16)
        o_ref[hh, :, HEAD_DIM:] = onehot


def _store_transposed(blk, o_ref, hh):
    blk_t = blk.T.astype(BF16)
    for pp in range(o_ref.shape[1]):
        o_ref[hh, pp, :HEAD_DIM] = blk_t[:, pp * PAIR:(pp + 1) * PAIR]


def _qt_kernel(h_ref, w_ref, nw_ref, o_ref, *, heads_per_tile):
    res = jnp.dot(h_ref[...], w_ref[...], preferred_element_type=F32)
    for hh in range(heads_per_tile):
        blk = _head_norm(res[:, hh * HEAD_DIM:(hh + 1) * HEAD_DIM], nw_ref[...])
        _store_transposed(blk * QK_SCALE_LOG2, o_ref, hh)


def _vt_kernel(h_ref, w_ref, o_ref, *, heads_per_tile):
    res = jnp.dot(h_ref[...], w_ref[...], preferred_element_type=F32)
    for hh in range(heads_per_tile):
        _store_transposed(res[:, hh * HEAD_DIM:(hh + 1) * HEAD_DIM], o_ref, hh)
        o_ref[hh, :, HEAD_DIM:] = jnp.ones((o_ref.shape[1], V_ROWS - HEAD_DIM, PAIR), BF16)


def _head_proj(h, w, col0, n, norm_w, seq, kind):
    m, d = h.shape
    tm = _pick(seq, 1024)
    tn = _pick(n, 1024)
    assert col0 % tn == 0
    hpt = tn // HEAD_DIM
    in_specs = [pl.BlockSpec((tm, d), lambda i, j: (i, 0)),
                pl.BlockSpec((d, tn), lambda i, j: (0, col0 // tn + j))]
    args = [h, w]
    if kind != "v":
        in_specs.append(pl.BlockSpec((1, HEAD_DIM), lambda i, j: (0, 0)))
        args.append(norm_w)
    if kind == "k":
        body = functools.partial(_k_kernel, heads_per_tile=hpt, seq=seq)
        out_spec = pl.BlockSpec((hpt, tm, 2 * HEAD_DIM), lambda i, j: (j, i, 0))
        out_shape = (n // HEAD_DIM, m, 2 * HEAD_DIM)
    else:
        body = functools.partial(_qt_kernel if kind == "q" else _vt_kernel, heads_per_tile=hpt)
        rows = HEAD_DIM if kind == "q" else V_ROWS
        out_spec = pl.BlockSpec((hpt, tm // PAIR, rows, PAIR), lambda i, j: (j, i, 0, 0))
        out_shape = (n // HEAD_DIM, m // PAIR, rows, PAIR)
    return pl.pallas_call(
        body,
        grid=(m // tm, n // tn),
        in_specs=in_specs,
        out_specs=out_spec,
        out_shape=jax.ShapeDtypeStruct(out_shape, BF16),
        compiler_params=_params("parallel", "arbitrary"),
        name=f"{kind}_proj",
    )(*args)


def _conv_kernel(h_ref, wc_ref, wb_ref, wh_ref, cw_ref, *refs, tiles_per_batch, n_cast):
    cast_in, o_ref, cast_out, ubuf_ref = refs[:n_cast], refs[n_cast], refs[n_cast + 1:2 * n_cast + 1], refs[-1]
    _cast_streams(cast_in, cast_out)
    i = pl.program_id(1)
    tm = h_ref.shape[0]

    @pl.when(i % tiles_per_batch == 0)
    def _():
        ubuf_ref[0:SUBLANES, :] = jnp.zeros((SUBLANES, ubuf_ref.shape[1]), F32)

    h = h_ref[...]
    cg = jnp.dot(h, wc_ref[...], preferred_element_type=F32)
    hc = jnp.dot(h, wh_ref[...], preferred_element_type=F32)
    ubuf_ref[SUBLANES:SUBLANES + tm, :] = cg * hc
    u0 = ubuf_ref[SUBLANES:SUBLANES + tm, :]
    u1 = ubuf_ref[SUBLANES - 1:SUBLANES - 1 + tm, :]
    u2 = ubuf_ref[SUBLANES - 2:SUBLANES - 2 + tm, :]
    y = cw_ref[0:1, :] * u2 + cw_ref[1:2, :] * u1 + cw_ref[2:3, :] * u0
    bg = jnp.dot(h, wb_ref[...], preferred_element_type=F32)
    o_ref[...] = (bg * y).astype(BF16)
    ubuf_ref[0:SUBLANES, :] = ubuf_ref[tm:tm + SUBLANES, :]


def _conv(h, w, col0, cw, conv_w, seq, to_cast):
    m, d = h.shape
    tm = _pick(seq, 1024)
    tc = _pick(cw, 256)
    assert col0 % tc == 0
    n_i = m // tm
    wspec = lambda group: pl.BlockSpec((d, tc), lambda c, i: (0, (col0 + group * cw) // tc + c))
    cast_in, cast_out, cast_shapes = _cast_plan(to_cast, (cw // tc) * n_i, lambda c, i: c * n_i + i)
    outs = pl.pallas_call(
        functools.partial(_conv_kernel, tiles_per_batch=seq // tm, n_cast=len(to_cast)),
        grid=(cw // tc, n_i),
        in_specs=[pl.BlockSpec((tm, d), lambda c, i: (i, 0)), wspec(0), wspec(1), wspec(2),
                  pl.BlockSpec((CONV_WIDTH, tc), lambda c, i: (0, c))] + cast_in,
        out_specs=[pl.BlockSpec((tm, tc), lambda c, i: (i, c))] + cast_out,
        out_shape=[jax.ShapeDtypeStruct((m, cw), BF16)] + cast_shapes,
        scratch_shapes=[pltpu.VMEM((tm + SUBLANES, tc), F32)],
        compiler_params=_params("arbitrary", "arbitrary"),
        name="conv_proj",
    )(h, w, w, w, conv_w, *[a for a, _ in to_cast])
    return outs[0], outs[1:]


def _t5_bucket_np(dist):
    max_exact = N_BUCKETS // 2
    d = np.maximum(dist, 1).astype(np.float32)
    large = max_exact + (np.log(d / np.float32(max_exact)) / np.float32(math.log(MAX_DISTANCE / max_exact))
                         * np.float32(N_BUCKETS - max_exact)).astype(np.int32)
    large = np.minimum(large, N_BUCKETS - 1)
    return np.where(dist < max_exact, dist, large)


def _bias_plan(seq):
    buckets = _t5_bucket_np(np.arange(seq, dtype=np.int64))
    assert (np.diff(buckets) >= 0).all()
    thr = [int(np.argmax(buckets >= b)) if (buckets >= b).any() else seq for b in range(N_BUCKETS)]
    nb = seq // BLOCK
    n_tiles = nb
    for delta in range(1, nb):
        if buckets[delta * BLOCK - (BLOCK - 1)] == buckets[-1]:
            n_tiles = delta + 1
            break
    ranges = []
    for delta in range(n_tiles):
        lo = max(0, delta * BLOCK - (BLOCK - 1))
        hi = min(seq - 1, delta * BLOCK + (BLOCK - 1))
        ranges.append((int(buckets[lo]), int(buckets[hi])))
    return thr, ranges


def _work_items(nb):
    items = [(u, jj) for u in range(nb // 2) for jj in range(u + 1)]
    tab = np.zeros((3, len(items)), np.int32)
    for t, (u, jj) in enumerate(items):
        tab[0, t], tab[1, t] = u, jj
        tab[2, t] = 1 if t >= 1 and items[t - 1][1] == items[t - 1][0] else 0
    return tab


def _attn_kernel(tbl_ref, item_ref, qt_ref, k_ref, vt_ref, o_ref, bias_ref, kmean_ref, qa_ref,
                 logit0_ref, logit1_ref, p0_ref, p1_ref, *, nb, thr, ranges, n_steps):
    hd = pl.program_id(0)
    b = pl.program_id(1)
    n_tiles = len(ranges)
    n_sb = nb // 2

    @pl.when(b == 0)
    def _build_bias():
        rel = (lax.broadcasted_iota(jnp.int32, (BLOCK, BLOCK), 1)
               - lax.broadcasted_iota(jnp.int32, (BLOCK, BLOCK), 0))
        for delta, (b_lo, b_hi) in enumerate(ranges):
            dist = rel + delta * BLOCK
            val = jnp.full((BLOCK, BLOCK), tbl_ref[hd, b_lo], F32)
            for bk in range(b_lo + 1, b_hi + 1):
                val = jnp.where(dist >= thr[bk], tbl_ref[hd, bk], val)
            val = val * LOG2E
            if delta == 0:
                val = jnp.where(rel >= 0, val, NEG)
            bias_ref[delta] = val

    for j in range(nb):
        kj = k_ref[j * BLOCK:(j + 1) * BLOCK, :HEAD_DIM].astype(F32)
        kmean_ref[j:j + 1, :] = jnp.sum(kj, axis=0, keepdims=True) * (1.0 / BLOCK)
    kmean = kmean_ref[...].astype(BF16)

    blk_id = lax.broadcasted_iota(jnp.int32, (nb, PAIR), 0)
    q_half = (lax.broadcasted_iota(jnp.int32, (1, PAIR), 1) >= BLOCK).astype(jnp.int32)

    def build_qa(u, carry):
        q_t = qt_ref[u]
        q_blk = 2 * u + q_half
        gate = jnp.dot(kmean, q_t, preferred_element_type=F32)
        rank = jnp.zeros((nb, PAIR), jnp.int32)
        for jp in range(nb):
            live = (q_blk > jp).astype(jnp.int32)
            tie = jnp.where(blk_id > jp, live, 0)
            g_jp = gate[jp:jp + 1, :]
            rank = rank + jnp.where(g_jp > gate, live, jnp.where(g_jp == gate, tie, 0))
        keep = jnp.where(blk_id < q_blk, rank, jnp.where(blk_id == q_blk, 0, TOPK)) < TOPK
        qa_ref[u, :HEAD_DIM] = q_t
        qa_ref[u, HEAD_DIM:HEAD_DIM + nb] = jnp.where(keep, 0.0, NEG).astype(BF16)
        return carry

    qa_ref[:, HEAD_DIM + nb:] = jnp.zeros((n_sb, LANES - nb, PAIR), BF16)
    for u in range(n_sb):
        build_qa(u, 0)

    def qk_logits(t, logit_ref):
        u = item_ref[0, t]
        jj = item_ref[1, t]
        r0 = pl.multiple_of(jj * PAIR, PAIR)
        raw = jnp.dot(k_ref[pl.ds(r0, PAIR), :], qa_ref[u], preferred_element_type=F32)
        d0 = 2 * (u - jj)
        tile = lambda d: bias_ref[jnp.clip(d, 0, n_tiles - 1)]
        top = raw[:BLOCK] + jnp.concatenate([tile(d0), tile(d0 + 1)], axis=1)
        bot = raw[BLOCK:] + jnp.concatenate([tile(d0 - 1), tile(d0)], axis=1)
        logit_ref[:BLOCK] = top
        logit_ref[BLOCK:] = bot
        return jnp.maximum(jnp.max(top, axis=0, keepdims=True), jnp.max(bot, axis=0, keepdims=True))

    def pv(t_prev, alpha_prev, acc, p_prev):
        return alpha_prev * acc + jnp.dot(vt_ref[item_ref[1, t_prev]], p_prev[...], preferred_element_type=F32)

    def step(t, c, logit_cur, logit_next, p_cur, p_prev):
        cmax, alpha_prev, m, acc = c
        cmax_next = qk_logits(jnp.minimum(t + 1, n_steps - 1), logit_next)
        acc = pv(jnp.maximum(t - 1, 0), alpha_prev, acc, p_prev)

        m_old = jnp.where(item_ref[1, t] == 0, NEG, m)
        m_new = jnp.maximum(m_old, cmax)
        alpha = jnp.exp2(m_old - m_new)
        p_cur[...] = jnp.exp2(logit_cur[...] - m_new).astype(BF16)
        return (cmax_next, alpha, m_new, acc), (t, acc)

    def store(u, acc):
        r0 = pl.multiple_of(u * PAIR, PAIR)
        o_ref[pl.ds(r0, PAIR), :] = (acc[:HEAD_DIM] * (1.0 / acc[HEAD_DIM:HEAD_DIM + 1])).T.astype(BF16)

    def finish(done):
        t, acc = done
        pl.when(item_ref[2, t] == 1)(lambda: store(item_ref[0, jnp.maximum(t - 1, 0)], acc))

    logit_refs = (logit0_ref, logit1_ref)
    p_refs = (p0_ref, p1_ref)

    def steps(t0, count, c):
        done = []
        for s in range(count):
            c, d = step(t0 + s, c, logit_refs[s % 2], logit_refs[1 - s % 2], p_refs[s % 2], p_refs[1 - s % 2])
            done.append(d)
        for d in done:
            finish(d)
        return c

    p1_ref[...] = jnp.zeros_like(p1_ref)
    c = (qk_logits(0, logit0_ref), jnp.zeros((1, PAIR), F32),
         jnp.full((1, PAIR), NEG, F32), jnp.zeros((V_ROWS, PAIR), F32))
    n_main = n_steps // ATTN_UNROLL
    c = lax.fori_loop(0, n_main, lambda tt, c: steps(ATTN_UNROLL * tt, ATTN_UNROLL, c), c)
    _, alpha_prev, _, acc = steps(n_main * ATTN_UNROLL, n_steps % ATTN_UNROLL, c)
    store(n_sb - 1, pv(n_steps - 1, alpha_prev, acc, p_refs[(n_steps - 1) % 2]))


def _attention(qt, k, vt, rel_bias_t, batch, seq, n_heads):
    nb = seq // BLOCK
    assert nb % 2 == 0 and nb <= LANES
    thr, ranges = _bias_plan(seq)
    items = _work_items(nb)
    n_steps = items.shape[1]
    m = batch * seq
    t_spec = lambda rows: pl.BlockSpec((None, seq // PAIR, rows, PAIR), lambda h, b: (h, b, 0, 0))
    return pl.pallas_call(
        functools.partial(_attn_kernel, nb=nb, thr=thr, ranges=ranges, n_steps=n_steps),
        grid=(n_heads, batch),
        in_specs=[pl.BlockSpec(memory_space=pltpu.SMEM),
                  pl.BlockSpec(memory_space=pltpu.SMEM),
                  t_spec(HEAD_DIM),
                  pl.BlockSpec((None, seq, 2 * HEAD_DIM), lambda h, b: (h, b, 0)),
                  t_spec(V_ROWS)],
        out_specs=pl.BlockSpec((seq, HEAD_DIM), lambda h, b: (b, h)),
        out_shape=jax.ShapeDtypeStruct((m, n_heads * HEAD_DIM), BF16),
        scratch_shapes=[pltpu.VMEM((len(ranges), BLOCK, BLOCK), F32),
                        pltpu.VMEM((nb, HEAD_DIM), F32),
                        pltpu.VMEM((nb // 2, 2 * LANES, PAIR), BF16),
                        pltpu.VMEM((PAIR, PAIR), F32),
                        pltpu.VMEM((PAIR, PAIR), F32),
                        pltpu.VMEM((PAIR, PAIR), BF16),
                        pltpu.VMEM((PAIR, PAIR), BF16)],
        compiler_params=_params("arbitrary", "arbitrary"),
        name="moba_attn",
    )(rel_bias_t, jnp.asarray(items), qt, k, vt)


def _outproj_kernel(attn_ref, conv_ref, wa_ref, wc_ref, w_ref, x_ref, ada_ref, o_ref, y_ref, *, k):
    j = pl.program_id(1)
    aw = attn_ref.shape[1]

    @pl.when(j == 0)
    def _():
        a = attn_ref[...].astype(F32)
        y_ref[:, :aw] = (a * lax.rsqrt(jnp.mean(a * a, axis=-1, keepdims=True) + EPS) * wa_ref[...]).astype(BF16)
        c = conv_ref[...].astype(F32)
        y_ref[:, aw:] = (c * lax.rsqrt(jnp.mean(c * c, axis=-1, keepdims=True) + EPS) * wc_ref[...]).astype(BF16)

    res = jnp.dot(y_ref[...], w_ref[...], preferred_element_type=F32)
    o_ref[...] = x_ref[...] + ada_ref[3 * k + 2:3 * k + 3, :] * res


def _outproj(attn, conv, wa, wc, w_out, x2d, ada, k, seq):
    m, aw = attn.shape
    cw = conv.shape[1]
    d = w_out.shape[1]
    tm = _pick(seq, 1024)
    tn = _pick(d, 512)
    return pl.pallas_call(
        functools.partial(_outproj_kernel, k=k),
        grid=(m // tm, d // tn),
        in_specs=[pl.BlockSpec((tm, aw), lambda i, j: (i, 0)),
                  pl.BlockSpec((tm, cw), lambda i, j: (i, 0)),
                  pl.BlockSpec((1, aw), lambda i, j: (0, 0)),
                  pl.BlockSpec((1, cw), lambda i, j: (0, 0)),
                  pl.BlockSpec((aw + cw, tn), lambda i, j: (0, j)),
                  pl.BlockSpec((tm, tn), lambda i, j: (i, j)),
                  pl.BlockSpec((None, N_ADA, tn), lambda i, j: (i * tm // seq, 0, j))],
        out_specs=pl.BlockSpec((tm, tn), lambda i, j: (i, j)),
        out_shape=jax.ShapeDtypeStruct((m, d), F32),
        scratch_shapes=[pltpu.VMEM((tm, aw + cw), BF16)],
        compiler_params=_params("parallel", "arbitrary"),
        name="out_proj",
    )(attn, conv, wa, wc, w_out, x2d, ada)


def _pad_to(a, axis, mult):
    pad = -a.shape[axis] % mult
    if not pad:
        return a
    widths = [(0, 0)] * a.ndim
    widths[axis] = (0, pad)
    return jnp.pad(a, widths)


def kernel(x, c, ada_w, ada_b, ffn1_norm, ffn1_w_gate, ffn1_w_up, ffn1_w_down, mix_norm, w_in, q_norm, k_norm,
           rel_bias, conv_w, attn_out_norm, conv_out_norm, w_out, ffn2_norm, ffn2_w_gate, ffn2_w_up, ffn2_w_down):
    batch, seq, d = x.shape
    depth = ada_w.shape[0]
    mix_w = w_out.shape[1]
    att_w = mix_w // 2
    conv_cw = mix_w - att_w
    n_heads = att_w // HEAD_DIM
    assert seq % BLOCK == 0 and att_w % HEAD_DIM == 0

    xs = x.reshape(batch * seq, d)
    c_pad = _pad_to(c, 0, SUBLANES)
    rel_bias_t = rel_bias.T
    for l in range(depth):
        ada = _ada(c_pad, ada_w[l], ada_b[l][None, :])[:batch].reshape(batch, N_ADA, d)

        h = _norm_mod(xs, ffn1_norm[l][None, :], ada, 0, seq)
        xs, (w_in_l,) = _ffn(h, ffn1_w_gate[l].astype(BF16), ffn1_w_up[l].astype(BF16), ffn1_w_down[l].astype(BF16),
                             xs, ada, 0, seq, to_cast=[(w_in[l], None)])

        h = _norm_mod(xs, mix_norm[l][None, :], ada, 1, seq)
        qt = _head_proj(h, w_in_l, 0, att_w, q_norm[l][None, :], seq, "q")
        kk = _head_proj(h, w_in_l, att_w, att_w, k_norm[l][None, :], seq, "k")
        vt = _head_proj(h, w_in_l, 2 * att_w, att_w, None, seq, "v")
        conv, (wg2, wu2, wd2, w_out_l) = _conv(h, w_in_l, 3 * att_w, conv_cw, conv_w[l], seq,
                                               [(ffn2_w_gate[l], FFN_TILE), (ffn2_w_up[l], FFN_TILE),
                                                (ffn2_w_down[l], None), (w_out[l], None)])
        attn = _attention(qt, kk, vt, rel_bias_t, batch, seq, n_heads)
        xs = _outproj(attn, conv, attn_out_norm[l][None, :], conv_out_norm[l][None, :], w_out_l, xs, ada, 1, seq)

        h = _norm_mod(xs, ffn2_norm[l][None, :], ada, 2, seq)
        xs, _ = _ffn(h, wg2, wu2, wd2, xs, ada, 2, seq)
    return xs.reshape(batch, seq, d)
```

```python
import functools
import math

import numpy as np
import jax
import jax.numpy as jnp
from jax import lax
from jax.experimental import pallas as pl
from jax.experimental.pallas import tpu as pltpu

HEAD_DIM = 128
BLOCK = 256
TOPK = 3
CONV_WIDTH = 3
N_BUCKETS = 32
MAX_DISTANCE = 2048
N_ADA = 9
EPS = 1e-6
NEG = -1e30
PAIR = 2 * BLOCK
ATTN_UNROLL = 6
V_ROWS = HEAD_DIM + 16
FFN_TILE = 512
LOG2E = 1.4426950408889634
QK_SCALE_LOG2 = HEAD_DIM ** -0.5 * LOG2E

LANES = 128
SUBLANES = 8
VMEM_LIMIT_BYTES = 60000 * 1024

F32 = jnp.float32
BF16 = jnp.bfloat16


def _params(*sem):
    return pltpu.CompilerParams(dimension_semantics=sem, vmem_limit_bytes=VMEM_LIMIT_BYTES)


def _pick(n, pref):
    if n <= pref:
        return n
    t = pref
    while n % t:
        t //= 2
    return t


def _cast_plan(to_cast, n_steps, step_of):
    in_specs, out_specs, out_shapes = [], [], []
    for a, tile in to_cast:
        rows = -(-a.shape[0] // n_steps)
        rows = -(-rows // (2 * SUBLANES)) * 2 * SUBLANES
        while a.shape[0] % rows:
            rows += 2 * SUBLANES
        block = lambda *g, last=a.shape[0] // rows - 1: jnp.minimum(step_of(*g), last)
        in_specs.append(pl.BlockSpec((rows, a.shape[1]), lambda *g, block=block: (block(*g), 0)))
        if tile is None:
            out_specs.append(in_specs[-1])
            out_shapes.append(jax.ShapeDtypeStruct(a.shape, BF16))
        else:
            n_tiles = -(-a.shape[1] // tile)
            out_specs.append(pl.BlockSpec((n_tiles, rows, tile), lambda *g, block=block: (0, block(*g), 0)))
            out_shapes.append(jax.ShapeDtypeStruct((n_tiles, a.shape[0], tile), BF16))
    return in_specs, out_specs, out_shapes


def _cast_streams(cast_in, cast_out):
    for src, dst in zip(cast_in, cast_out):
        if len(dst.shape) == 2:
            dst[...] = src[...].astype(BF16)
            continue
        n_tiles, rows, tile = dst.shape
        for t in range(n_tiles):
            width = min(tile, src.shape[1] - t * tile)
            dst[t, :, :width] = src[:, t * tile:t * tile + width].astype(BF16)
            if width < tile:
                dst[t, :, width:] = jnp.zeros((rows, tile - width), BF16)


def _ada_kernel(c_ref, w_ref, b_ref, o_ref):
    c = c_ref[...]
    cond = (c * jax.nn.sigmoid(c)).astype(BF16)
    o_ref[...] = jnp.dot(cond, w_ref[...].astype(BF16), preferred_element_type=F32) + b_ref[...]


def _ada(c_pad, ada_w, ada_b):
    rows, d = c_pad.shape
    n = ada_w.shape[1]
    tn = _pick(n, 512)
    return pl.pallas_call(
        _ada_kernel,
        grid=(n // tn,),
        in_specs=[pl.BlockSpec((rows, d), lambda j: (0, 0)),
                  pl.BlockSpec((d, tn), lambda j: (0, j)),
                  pl.BlockSpec((1, tn), lambda j: (0, j))],
        out_specs=pl.BlockSpec((rows, tn), lambda j: (0, j)),
        out_shape=jax.ShapeDtypeStruct((rows, n), F32),
        compiler_params=_params("parallel"),
        name="ada_proj",
    )(c_pad, ada_w, ada_b)


def _norm_mod_kernel(x_ref, w_ref, ada_ref, o_ref, *, k):
    x = x_ref[...]
    ms = jnp.mean(x * x, axis=-1, keepdims=True)
    y = x * lax.rsqrt(ms + EPS) * w_ref[...]
    shift = ada_ref[3 * k:3 * k + 1, :]
    scale = ada_ref[3 * k + 1:3 * k + 2, :]
    o_ref[...] = (y * (1.0 + scale) + shift).astype(BF16)


def _norm_mod(x2d, w, ada, k, seq):
    m, d = x2d.shape
    tr = _pick(seq, 512)
    return pl.pallas_call(
        functools.partial(_norm_mod_kernel, k=k),
        grid=(m // tr,),
        in_specs=[pl.BlockSpec((tr, d), lambda i: (i, 0)),
                  pl.BlockSpec((1, d), lambda i: (0, 0)),
                  pl.BlockSpec((None, N_ADA, d), lambda i: (i * tr // seq, 0, 0))],
        out_specs=pl.BlockSpec((tr, d), lambda i: (i, 0)),
        out_shape=jax.ShapeDtypeStruct((m, d), BF16),
        compiler_params=_params("parallel"),
        name=f"norm_mod{k}",
    )(x2d, w, ada)


def _ffn_kernel(h_ref, wg_ref, wu_ref, wd_ref, x_ref, ada_ref, *refs, k, n_xchunks, xc, nchunk, tail, n_cast):
    cast_in, o_ref, cast_out = refs[:n_cast], refs[n_cast], refs[n_cast + 1:]
    _cast_streams(cast_in, cast_out)
    f = pl.program_id(1)
    nf = pl.num_programs(1)
    d = o_ref.shape[1]
    tf = wg_ref.shape[1]

    @pl.when(f == 0)
    def _():
        o_ref[...] = jnp.zeros_like(o_ref)

    for c in range(n_xchunks):
        @pl.when(f == c)
        def _(c=c):
            o_ref[:, c * xc:(c + 1) * xc] += x_ref[...]

    def accumulate(width):
        h = h_ref[...]
        g = jnp.dot(h, wg_ref[:, :width], preferred_element_type=F32)
        u = jnp.dot(h, wu_ref[:, :width], preferred_element_type=F32)
        a = (g * jax.nn.sigmoid(g) * u).astype(BF16)
        half_gate = 0.5 * ada_ref[3 * k + 2:3 * k + 3, :]
        for n0 in range(0, d, nchunk):
            part = jnp.dot(a, wd_ref[:width, n0:n0 + nchunk], preferred_element_type=F32)
            o_ref[:, n0:n0 + nchunk] += half_gate[:, n0:n0 + nchunk] * part

    if tail == tf:
        accumulate(tf)
    else:
        pl.when(f < nf - 1)(lambda: accumulate(tf))
        pl.when(f == nf - 1)(lambda: accumulate(tail))


def _ffn(h, wg, wu, wd, x2d, ada, k, seq, to_cast=(), tm_pref=512):
    m, d = h.shape
    d_ff = wd.shape[0]
    tm = _pick(seq, tm_pref)
    tf = min(FFN_TILE, d_ff)
    nf = pl.cdiv(d_ff, tf)
    tail = d_ff - (nf - 1) * tf
    assert tail % LANES == 0, (d_ff, tf)
    n_xchunks = 1
    while d % n_xchunks or (d // n_xchunks) % LANES or d // n_xchunks > 512:
        n_xchunks += 1
    assert n_xchunks <= nf, (n_xchunks, nf)
    xc = d // n_xchunks
    nchunk = _pick(d, 512)
    cast_in, cast_out, cast_shapes = _cast_plan(to_cast, (m // tm) * nf, lambda i, f: i * nf + f)
    if wg.ndim == 3:
        assert wg.shape == wu.shape == (nf, d, tf), (wg.shape, wu.shape)
        w_spec = pl.BlockSpec((None, d, tf), lambda i, f: (f, 0, 0))
    else:
        w_spec = pl.BlockSpec((d, tf), lambda i, f: (0, f))
    outs = pl.pallas_call(
        functools.partial(_ffn_kernel, k=k, n_xchunks=n_xchunks, xc=xc, nchunk=nchunk, tail=tail,
                          n_cast=len(to_cast)),
        grid=(m // tm, nf),
        in_specs=[pl.BlockSpec((tm, d), lambda i, f: (i, 0)),
                  w_spec,
                  w_spec,
                  pl.BlockSpec((tf, d), lambda i, f: (f, 0)),
                  pl.BlockSpec((tm, xc), lambda i, f: (i, jnp.minimum(f, n_xchunks - 1))),
                  pl.BlockSpec((None, N_ADA, d), lambda i, f: (i * tm // seq, 0, 0))] + cast_in,
        out_specs=[pl.BlockSpec((tm, d), lambda i, f: (i, 0))] + cast_out,
        out_shape=[jax.ShapeDtypeStruct((m, d), F32)] + cast_shapes,
        compiler_params=_params("arbitrary", "arbitrary"),
        name=f"ffn{k}",
    )(h, wg, wu, wd, x2d, ada, *[a for a, _ in to_cast])
    return outs[0], outs[1:]


def _head_norm(blk, w):
    ms = jnp.mean(blk * blk, axis=-1, keepdims=True)
    return blk * lax.rsqrt(ms + EPS) * w


def _k_kernel(h_ref, w_ref, nw_ref, o_ref, *, heads_per_tile, seq):
    tm = h_ref.shape[0]
    res = jnp.dot(h_ref[...], w_ref[...], preferred_element_type=F32)
    pos = (pl.program_id(0) * tm) % seq + lax.broadcasted_iota(jnp.int32, (tm, LANES), 0)
    lane = lax.broadcasted_iota(jnp.int32, (tm, LANES), 1)
    onehot = jnp.where(lane * BLOCK == pos - pos % BLOCK, 1.0, 0.0).astype(BF16)
    for hh in range(heads_per_tile):
        o_ref[hh, :, :HEAD_DIM] = _head_norm(res[:, hh * HEAD_DIM:(hh + 1) * HEAD_DIM], nw_ref[...]).astype(BF16)
        o_ref[hh, :, HEAD_DIM:] = onehot


def _store_transposed(blk, o_ref, hh):
    blk_t = blk.T.astype(BF16)
    for pp in range(o_ref.shape[1]):
        o_ref[hh, pp, :HEAD_DIM] = blk_t[:, pp * PAIR:(pp + 1) * PAIR]


def _qt_kernel(h_ref, w_ref, nw_ref, o_ref, *, heads_per_tile):
    res = jnp.dot(h_ref[...], w_ref[...], preferred_element_type=F32)
    for hh in range(heads_per_tile):
        blk = _head_norm(res[:, hh * HEAD_DIM:(hh + 1) * HEAD_DIM], nw_ref[...])
        _store_transposed(blk * QK_SCALE_LOG2, o_ref, hh)


def _vt_kernel(h_ref, w_ref, o_ref, *, heads_per_tile):
    res = jnp.dot(h_ref[...], w_ref[...], preferred_element_type=F32)
    for hh in range(heads_per_tile):
        _store_transposed(res[:, hh * HEAD_DIM:(hh + 1) * HEAD_DIM], o_ref, hh)
        o_ref[hh, :, HEAD_DIM:] = jnp.ones((o_ref.shape[1], V_ROWS - HEAD_DIM, PAIR), BF16)


def _head_proj(h, w, col0, n, norm_w, seq, kind):
    m, d = h.shape
    tm = _pick(seq, 1024)
    tn = _pick(n, 1024)
    assert col0 % tn == 0
    hpt = tn // HEAD_DIM
    in_specs = [pl.BlockSpec((tm, d), lambda i, j: (i, 0)),
                pl.BlockSpec((d, tn), lambda i, j: (0, col0 // tn + j))]
    args = [h, w]
    if kind != "v":
        in_specs.append(pl.BlockSpec((1, HEAD_DIM), lambda i, j: (0, 0)))
        args.append(norm_w)
    if kind == "k":
        body = functools.partial(_k_kernel, heads_per_tile=hpt, seq=seq)
        out_spec = pl.BlockSpec((hpt, tm, 2 * HEAD_DIM), lambda i, j: (j, i, 0))
        out_shape = (n // HEAD_DIM, m, 2 * HEAD_DIM)
    else:
        body = functools.partial(_qt_kernel if kind == "q" else _vt_kernel, heads_per_tile=hpt)
        rows = HEAD_DIM if kind == "q" else V_ROWS
        out_spec = pl.BlockSpec((hpt, tm // PAIR, rows, PAIR), lambda i, j: (j, i, 0, 0))
        out_shape = (n // HEAD_DIM, m // PAIR, rows, PAIR)
    return pl.pallas_call(
        body,
        grid=(m // tm, n // tn),
        in_specs=in_specs,
        out_specs=out_spec,
        out_shape=jax.ShapeDtypeStruct(out_shape, BF16),
        compiler_params=_params("parallel", "arbitrary"),
        name=f"{kind}_proj",
    )(*args)


def _conv_kernel(h_ref, wc_ref, wb_ref, wh_ref, cw_ref, *refs, tiles_per_batch, n_cast):
    cast_in, o_ref, cast_out, ubuf_ref = refs[:n_cast], refs[n_cast], refs[n_cast + 1:2 * n_cast + 1], refs[-1]
    _cast_streams(cast_in, cast_out)
    i = pl.program_id(1)
    tm = h_ref.shape[0]

    @pl.when(i % tiles_per_batch == 0)
    def _():
        ubuf_ref[0:SUBLANES, :] = jnp.zeros((SUBLANES, ubuf_ref.shape[1]), F32)

    h = h_ref[...]
    cg = jnp.dot(h, wc_ref[...], preferred_element_type=F32)
    hc = jnp.dot(h, wh_ref[...], preferred_element_type=F32)
    ubuf_ref[SUBLANES:SUBLANES + tm, :] = cg * hc
    u0 = ubuf_ref[SUBLANES:SUBLANES + tm, :]
    u1 = ubuf_ref[SUBLANES - 1:SUBLANES - 1 + tm, :]
    u2 = ubuf_ref[SUBLANES - 2:SUBLANES - 2 + tm, :]
    y = cw_ref[0:1, :] * u2 + cw_ref[1:2, :] * u1 + cw_ref[2:3, :] * u0
    bg = jnp.dot(h, wb_ref[...], preferred_element_type=F32)
    o_ref[...] = (bg * y).astype(BF16)
    ubuf_ref[0:SUBLANES, :] = ubuf_ref[tm:tm + SUBLANES, :]


def _conv(h, w, col0, cw, conv_w, seq, to_cast):
    m, d = h.shape
    tm = _pick(seq, 1024)
    tc = _pick(cw, 256)
    assert col0 % tc == 0
    n_i = m // tm
    wspec = lambda group: pl.BlockSpec((d, tc), lambda c, i: (0, (col0 + group * cw) // tc + c))
    cast_in, cast_out, cast_shapes = _cast_plan(to_cast, (cw // tc) * n_i, lambda c, i: c * n_i + i)
    outs = pl.pallas_call(
        functools.partial(_conv_kernel, tiles_per_batch=seq // tm, n_cast=len(to_cast)),
        grid=(cw // tc, n_i),
        in_specs=[pl.BlockSpec((tm, d), lambda c, i: (i, 0)), wspec(0), wspec(1), wspec(2),
                  pl.BlockSpec((CONV_WIDTH, tc), lambda c, i: (0, c))] + cast_in,
        out_specs=[pl.BlockSpec((tm, tc), lambda c, i: (i, c))] + cast_out,
        out_shape=[jax.ShapeDtypeStruct((m, cw), BF16)] + cast_shapes,
        scratch_shapes=[pltpu.VMEM((tm + SUBLANES, tc), F32)],
        compiler_params=_params("arbitrary", "arbitrary"),
        name="conv_proj",
    )(h, w, w, w, conv_w, *[a for a, _ in to_cast])
    return outs[0], outs[1:]


def _t5_bucket_np(dist):
    max_exact = N_BUCKETS // 2
    d = np.maximum(dist, 1).astype(np.float32)
    large = max_exact + (np.log(d / np.float32(max_exact)) / np.float32(math.log(MAX_DISTANCE / max_exact))
                         * np.float32(N_BUCKETS - max_exact)).astype(np.int32)
    large = np.minimum(large, N_BUCKETS - 1)
    return np.where(dist < max_exact, dist, large)


def _bias_plan(seq):
    buckets = _t5_bucket_np(np.arange(seq, dtype=np.int64))
    assert (np.diff(buckets) >= 0).all()
    thr = [int(np.argmax(buckets >= b)) if (buckets >= b).any() else seq for b in range(N_BUCKETS)]
    nb = seq // BLOCK
    n_tiles = nb
    for delta in range(1, nb):
        if buckets[delta * BLOCK - (BLOCK - 1)] == buckets[-1]:
            n_tiles = delta + 1
            break
    ranges = []
    for delta in range(n_tiles):
        lo = max(0, delta * BLOCK - (BLOCK - 1))
        hi = min(seq - 1, delta * BLOCK + (BLOCK - 1))
        ranges.append((int(buckets[lo]), int(buckets[hi])))
    return thr, ranges


def _work_items(nb):
    items = [(u, jj) for u in range(nb // 2) for jj in range(u + 1)]
    tab = np.zeros((3, len(items)), np.int32)
    for t, (u, jj) in enumerate(items):
        tab[0, t], tab[1, t] = u, jj
        tab[2, t] = 1 if t >= 1 and items[t - 1][1] == items[t - 1][0] else 0
    return tab


def _attn_kernel(tbl_ref, item_ref, qt_ref, k_ref, vt_ref, o_ref, bias_ref, kmean_ref, qa_ref,
                 logit0_ref, logit1_ref, p0_ref, p1_ref, *, nb, thr, ranges, n_steps):
    hd = pl.program_id(0)
    b = pl.program_id(1)
    n_tiles = len(ranges)
    n_sb = nb // 2

    @pl.when(b == 0)
    def _build_bias():
        rel = (lax.broadcasted_iota(jnp.int32, (BLOCK, BLOCK), 1)
               - lax.broadcasted_iota(jnp.int32, (BLOCK, BLOCK), 0))
        for delta, (b_lo, b_hi) in enumerate(ranges):
            dist = rel + delta * BLOCK
            val = jnp.full((BLOCK, BLOCK), tbl_ref[hd, b_lo], F32)
            for bk in range(b_lo + 1, b_hi + 1):
                val = jnp.where(dist >= thr[bk], tbl_ref[hd, bk], val)
            val = val * LOG2E
            if delta == 0:
                val = jnp.where(rel >= 0, val, NEG)
            bias_ref[delta] = val

    for j in range(nb):
        kj = k_ref[j * BLOCK:(j + 1) * BLOCK, :HEAD_DIM].astype(F32)
        kmean_ref[j:j + 1, :] = jnp.sum(kj, axis=0, keepdims=True) * (1.0 / BLOCK)
    kmean = kmean_ref[...].astype(BF16)

    blk_id = lax.broadcasted_iota(jnp.int32, (nb, PAIR), 0)
    q_half = (lax.broadcasted_iota(jnp.int32, (1, PAIR), 1) >= BLOCK).astype(jnp.int32)

    def build_qa(u, carry):
        q_t = qt_ref[u]
        q_blk = 2 * u + q_half
        gate = jnp.dot(kmean, q_t, preferred_element_type=F32)
        rank = jnp.zeros((nb, PAIR), jnp.int32)
        for jp in range(nb):
            live = (q_blk > jp).astype(jnp.int32)
            tie = jnp.where(blk_id > jp, live, 0)
            g_jp = gate[jp:jp + 1, :]
            rank = rank + jnp.where(g_jp > gate, live, jnp.where(g_jp == gate, tie, 0))
        keep = jnp.where(blk_id < q_blk, rank, jnp.where(blk_id == q_blk, 0, TOPK)) < TOPK
        qa_ref[u, :HEAD_DIM] = q_t
        qa_ref[u, HEAD_DIM:HEAD_DIM + nb] = jnp.where(keep, 0.0, NEG).astype(BF16)
        return carry

    qa_ref[:, HEAD_DIM + nb:] = jnp.zeros((n_sb, LANES - nb, PAIR), BF16)
    for u in range(n_sb):
        build_qa(u, 0)

    def qk_logits(t, logit_ref):
        u = item_ref[0, t]
        jj = item_ref[1, t]
        r0 = pl.multiple_of(jj * PAIR, PAIR)
        raw = jnp.dot(k_ref[pl.ds(r0, PAIR), :], qa_ref[u], preferred_element_type=F32)
        d0 = 2 * (u - jj)
        tile = lambda d: bias_ref[jnp.clip(d, 0, n_tiles - 1)]
        top = raw[:BLOCK] + jnp.concatenate([tile(d0), tile(d0 + 1)], axis=1)
        bot = raw[BLOCK:] + jnp.concatenate([tile(d0 - 1), tile(d0)], axis=1)
        logit_ref[:BLOCK] = top
        logit_ref[BLOCK:] = bot
        return jnp.maximum(jnp.max(top, axis=0, keepdims=True), jnp.max(bot, axis=0, keepdims=True))

    def pv(t_prev, alpha_prev, acc, p_prev):
        return alpha_prev * acc + jnp.dot(vt_ref[item_ref[1, t_prev]], p_prev[...], preferred_element_type=F32)

    def step(t, c, logit_cur, logit_next, p_cur, p_prev):
        cmax, alpha_prev, m, acc = c
        cmax_next = qk_logits(jnp.minimum(t + 1, n_steps - 1), logit_next)
        acc = pv(jnp.maximum(t - 1, 0), alpha_prev, acc, p_prev)

        m_old = jnp.where(item_ref[1, t] == 0, NEG, m)
        m_new = jnp.maximum(m_old, cmax)
        alpha = jnp.exp2(m_old - m_new)
        p_cur[...] = jnp.exp2(logit_cur[...] - m_new).astype(BF16)
        return (cmax_next, alpha, m_new, acc), (t, acc)

    def store(u, acc):
        r0 = pl.multiple_of(u * PAIR, PAIR)
        o_ref[pl.ds(r0, PAIR), :] = (acc[:HEAD_DIM] * (1.0 / acc[HEAD_DIM:HEAD_DIM + 1])).T.astype(BF16)

    def finish(done):
        t, acc = done
        pl.when(item_ref[2, t] == 1)(lambda: store(item_ref[0, jnp.maximum(t - 1, 0)], acc))

    logit_refs = (logit0_ref, logit1_ref)
    p_refs = (p0_ref, p1_ref)

    def steps(t0, count, c):
        done = []
        for s in range(count):
            c, d = step(t0 + s, c, logit_refs[s % 2], logit_refs[1 - s % 2], p_refs[s % 2], p_refs[1 - s % 2])
            done.append(d)
        for d in done:
            finish(d)
        return c

    p1_ref[...] = jnp.zeros_like(p1_ref)
    c = (qk_logits(0, logit0_ref), jnp.zeros((1, PAIR), F32),
         jnp.full((1, PAIR), NEG, F32), jnp.zeros((V_ROWS, PAIR), F32))
    n_main = n_steps // ATTN_UNROLL
    c = lax.fori_loop(0, n_main, lambda tt, c: steps(ATTN_UNROLL * tt, ATTN_UNROLL, c), c)
    _, alpha_prev, _, acc = steps(n_main * ATTN_UNROLL, n_steps % ATTN_UNROLL, c)
    store(n_sb - 1, pv(n_steps - 1, alpha_prev, acc, p_refs[(n_steps - 1) % 2]))


def _attention(qt, k, vt, rel_bias_t, batch, seq, n_heads):
    nb = seq // BLOCK
    assert nb % 2 == 0 and nb <= LANES
    thr, ranges = _bias_plan(seq)
    items = _work_items(nb)
    n_steps = items.shape[1]
    m = batch * seq
    t_spec = lambda rows: pl.BlockSpec((None, seq // PAIR, rows, PAIR), lambda h, b: (h, b, 0, 0))
    return pl.pallas_call(
        functools.partial(_attn_kernel, nb=nb, thr=thr, ranges=ranges, n_steps=n_steps),
        grid=(n_heads, batch),
        in_specs=[pl.BlockSpec(memory_space=pltpu.SMEM),
                  pl.BlockSpec(memory_space=pltpu.SMEM),
                  t_spec(HEAD_DIM),
                  pl.BlockSpec((None, seq, 2 * HEAD_DIM), lambda h, b: (h, b, 0)),
                  t_spec(V_ROWS)],
        out_specs=pl.BlockSpec((seq, HEAD_DIM), lambda h, b: (b, h)),
        out_shape=jax.ShapeDtypeStruct((m, n_heads * HEAD_DIM), BF16),
        scratch_shapes=[pltpu.VMEM((len(ranges), BLOCK, BLOCK), F32),
                        pltpu.VMEM((nb, HEAD_DIM), F32),
                        pltpu.VMEM((nb // 2, 2 * LANES, PAIR), BF16),
                        pltpu.VMEM((PAIR, PAIR), F32),
                        pltpu.VMEM((PAIR, PAIR), F32),
                        pltpu.VMEM((PAIR, PAIR), BF16),
                        pltpu.VMEM((PAIR, PAIR), BF16)],
        compiler_params=_params("arbitrary", "arbitrary"),
        name="moba_attn",
    )(rel_bias_t, jnp.asarray(items), qt, k, vt)


def _outproj_kernel(attn_ref, conv_ref, wa_ref, wc_ref, w_ref, x_ref, ada_ref, o_ref, y_ref, *, k):
    j = pl.program_id(1)
    aw = attn_ref.shape[1]

    @pl.when(j == 0)
    def _():
        a = attn_ref[...].astype(F32)
        y_ref[:, :aw] = (a * lax.rsqrt(jnp.mean(a * a, axis=-1, keepdims=True) + EPS) * wa_ref[...]).astype(BF16)
        c = conv_ref[...].astype(F32)
        y_ref[:, aw:] = (c * lax.rsqrt(jnp.mean(c * c, axis=-1, keepdims=True) + EPS) * wc_ref[...]).astype(BF16)

    res = jnp.dot(y_ref[...], w_ref[...], preferred_element_type=F32)
    o_ref[...] = x_ref[...] + ada_ref[3 * k + 2:3 * k + 3, :] * res


def _outproj(attn, conv, wa, wc, w_out, x2d, ada, k, seq):
    m, aw = attn.shape
    cw = conv.shape[1]
    d = w_out.shape[1]
    tm = _pick(seq, 1024)
    tn = _pick(d, 512)
    return pl.pallas_call(
        functools.partial(_outproj_kernel, k=k),
        grid=(m // tm, d // tn),
        in_specs=[pl.BlockSpec((tm, aw), lambda i, j: (i, 0)),
                  pl.BlockSpec((tm, cw), lambda i, j: (i, 0)),
                  pl.BlockSpec((1, aw), lambda i, j: (0, 0)),
                  pl.BlockSpec((1, cw), lambda i, j: (0, 0)),
                  pl.BlockSpec((aw + cw, tn), lambda i, j: (0, j)),
                  pl.BlockSpec((tm, tn), lambda i, j: (i, j)),
                  pl.BlockSpec((None, N_ADA, tn), lambda i, j: (i * tm // seq, 0, j))],
        out_specs=pl.BlockSpec((tm, tn), lambda i, j: (i, j)),
        out_shape=jax.ShapeDtypeStruct((m, d), F32),
        scratch_shapes=[pltpu.VMEM((tm, aw + cw), BF16)],
        compiler_params=_params("parallel", "arbitrary"),
        name="out_proj",
    )(attn, conv, wa, wc, w_out, x2d, ada)


def _pad_to(a, axis, mult):
    pad = -a.shape[axis] % mult
    if not pad:
        return a
    widths = [(0, 0)] * a.ndim
    widths[axis] = (0, pad)
    return jnp.pad(a, widths)


def kernel(x, c, ada_w, ada_b, ffn1_norm, ffn1_w_gate, ffn1_w_up, ffn1_w_down, mix_norm, w_in, q_norm, k_norm,
           rel_bias, conv_w, attn_out_norm, conv_out_norm, w_out, ffn2_norm, ffn2_w_gate, ffn2_w_up, ffn2_w_down):
    batch, seq, d = x.shape
    depth = ada_w.shape[0]
    mix_w = w_out.shape[1]
    att_w = mix_w // 2
    conv_cw = mix_w - att_w
    n_heads = att_w // HEAD_DIM
    assert seq % BLOCK == 0 and att_w % HEAD_DIM == 0

    xs = x.reshape(batch * seq, d)
    c_pad = _pad_to(c, 0, SUBLANES)
    rel_bias_t = rel_bias.T
    for l in range(depth):
        ada = _ada(c_pad, ada_w[l], ada_b[l][None, :])[:batch].reshape(batch, N_ADA, d)

        h = _norm_mod(xs, ffn1_norm[l][None, :], ada, 0, seq)
        xs, (w_in_l,) = _ffn(h, ffn1_w_gate[l].astype(BF16), ffn1_w_up[l].astype(BF16), ffn1_w_down[l].astype(BF16),
                             xs, ada, 0, seq, to_cast=[(w_in[l], None)])

        h = _norm_mod(xs, mix_norm[l][None, :], ada, 1, seq)
        qt = _head_proj(h, w_in_l, 0, att_w, q_norm[l][None, :], seq, "q")
        kk = _head_proj(h, w_in_l, att_w, att_w, k_norm[l][None, :], seq, "k")
        vt = _head_proj(h, w_in_l, 2 * att_w, att_w, None, seq, "v")
        conv, (wg2, wu2, wd2, w_out_l) = _conv(h, w_in_l, 3 * att_w, conv_cw, conv_w[l], seq,
                                               [(ffn2_w_gate[l], FFN_TILE), (ffn2_w_up[l], FFN_TILE),
                                                (ffn2_w_down[l], None), (w_out[l], None)])
        attn = _attention(qt, kk, vt, rel_bias_t, batch, seq, n_heads)
        xs = _outproj(attn, conv, attn_out_norm[l][None, :], conv_out_norm[l][None, :], w_out_l, xs, ada, 1, seq)

        h = _norm_mod(xs, ffn2_norm[l][None, :], ada, 2, seq)
        xs, _ = _ffn(h, wg2, wu2, wd2, xs, ada, 2, seq)
    return xs.reshape(batch, seq, d)
```
